```python
import math, functools
import jax, jax.numpy as jnp
from jax import lax
import numpy as np

D_MODEL = 1024
BATCH = 4
SEQ = 8192
DEPTH = 1
DEC_BATCH = 128
DEC_SEQ = 1
PAST_LEN = 8192
PAGE_SIZE = 128

D_CONV = 512
CONV_W = 3
N_HEADS = 8
HEAD_DIM = 64
D_ATTN = N_HEADS * HEAD_DIM
ATTN_SCALE = HEAD_DIM ** -0.5
N_IDX_HEADS = 8
IDX_DIM = 64
INDEX_SCALE = (N_IDX_HEADS * IDX_DIM) ** -0.5
TOPK_MAX = 256
Q_BLOCK = 128
N_BUCKETS = 32
MAX_DISTANCE = 128
N_EXPERTS = 32
TOP_K_EXPERTS = 4
D_FF = 1024
SWIGLU_LIMIT = 7.0
SWIGLU_ALPHA = 1.702
MOE_BLOCK = 128
EPS = 1e-6

OFF_CB = 0
OFF_CC = OFF_CB + D_CONV
OFF_CX = OFF_CC + D_CONV
OFF_Q = OFF_CX + D_CONV
OFF_K = OFF_Q + D_ATTN
OFF_V = OFF_K + D_ATTN
OFF_QI = OFF_V + D_ATTN
OFF_KI = OFF_QI + N_IDX_HEADS * IDX_DIM
OFF_WI = OFF_KI + IDX_DIM
OFF_GA = OFF_WI + N_IDX_HEADS
OFF_GB = OFF_GA + D_MODEL
D_IN_PROJ = OFF_GB + D_MODEL

kernel_name = 'hybrid_conv_dsa_moe_decode_step'


def rms_norm(x, g):
    xf = x.astype(jnp.float32)
    y = xf * lax.rsqrt(jnp.mean(xf * xf, axis=-1, keepdims=True) + EPS)
    return (y * g.astype(jnp.float32)).astype(x.dtype)


def t5_bucket(dist):
    n = jnp.maximum(dist, 0)
    max_exact = N_BUCKETS // 2
    nf = jnp.maximum(n, 1).astype(jnp.float32)
    large = max_exact + (jnp.log(nf / max_exact) / math.log(MAX_DISTANCE / max_exact)
                         * (N_BUCKETS - max_exact)).astype(jnp.int32)
    large = jnp.minimum(large, N_BUCKETS - 1)
    return jnp.where(n < max_exact, n, large)


def short_conv(b_gate, c_gate, xin, prefix, conv_w):
    u = c_gate * xin
    T = u.shape[1]
    padded = jnp.concatenate([prefix.astype(u.dtype), u], axis=1)
    y = sum(conv_w[j] * padded[:, j:j + T] for j in range(CONV_W))
    return b_gate * y, padded[:, T:]


def indexer_scores(qi, wi, kidx, q_pos):
    s = jnp.einsum('bthd,bsd->bths', qi, kidx).astype(jnp.float32)
    s = jnp.einsum('bths,bth->bts', jax.nn.relu(s), wi.astype(jnp.float32)) * INDEX_SCALE
    key_pos = jnp.arange(kidx.shape[1], dtype=jnp.int32)
    return jnp.where(key_pos[None, None, :] <= q_pos[None, :, None], s, -jnp.inf)


def attend_selected(q, q_pos, scores, topk, gather_kv, rel_bias):
    _, idx = lax.top_k(scores, topk)
    valid = idx <= q_pos[None, :, None]
    k_sel, v_sel = gather_kv(idx)
    logits = jnp.einsum('bthd,btkhd->bthk', q, k_sel).astype(jnp.float32) * ATTN_SCALE
    bias = rel_bias.astype(jnp.float32)[t5_bucket(q_pos[None, :, None] - idx)]
    logits = jnp.where(valid[:, :, None, :], logits + jnp.moveaxis(bias, -1, 2), -jnp.inf)
    p = jax.nn.softmax(logits, axis=-1).astype(v_sel.dtype)
    return jnp.einsum('bthk,btkhd->bthd', p, v_sel)


def prompt_attention(q, k, v, qi, wi, kidx, rel_bias):
    Bn, S = q.shape[:2]
    topk = min(TOPK_MAX, S // 4)
    nb = S // Q_BLOCK
    bidx = jnp.arange(Bn)[:, None, None]

    def gather_kv(idx):
        return k[bidx, idx], v[bidx, idx]

    def to_blocks(a):
        return jnp.swapaxes(a.reshape((Bn, nb, Q_BLOCK) + a.shape[2:]), 0, 1)

    pos = jnp.arange(S, dtype=jnp.int32).reshape(nb, Q_BLOCK)

    def one_block(args):
        qb, qib, wib, posb = args
        return attend_selected(qb, posb, indexer_scores(qib, wib, kidx, posb), topk, gather_kv, rel_bias)

    out = lax.map(one_block, (to_blocks(q), to_blocks(qi), to_blocks(wi), pos))
    return jnp.swapaxes(out, 0, 1).reshape(q.shape)


def sample_attention(q, k, v, qi, wi, kidx, cache_k, cache_v, cache_kidx, page_table, layer, rel_bias):
    Bn, T = q.shape[:2]
    page = cache_k.shape[2]
    past = page_table.shape[1] * page
    past_kidx = cache_kidx[layer, page_table].reshape(Bn, past, IDX_DIM).astype(kidx.dtype)
    kidx_all = jnp.concatenate([past_kidx, kidx], axis=1)
    topk = min(TOPK_MAX, (past + T) // 4)
    q_pos = past + jnp.arange(T, dtype=jnp.int32)
    bidx = jnp.arange(Bn)[:, None, None]

    def gather_kv(idx):
        pidx = jnp.clip(idx, 0, past - 1)
        phys = page_table[bidx, pidx // page]
        off = pidx % page
        nidx = jnp.clip(idx - past, 0, T - 1)
        is_new = (idx >= past)[..., None, None]
        k_sel = jnp.where(is_new, k[bidx, nidx], cache_k[layer, phys, off].astype(k.dtype))
        v_sel = jnp.where(is_new, v[bidx, nidx], cache_v[layer, phys, off].astype(v.dtype))
        return k_sel, v_sel

    return attend_selected(q, q_pos, indexer_scores(qi, wi, kidx_all, q_pos), topk, gather_kv, rel_bias)


def moe_ffn(x2d, w_router, b_router, w_gu, b_gu, w_dn, b_dn):
    N, D = x2d.shape
    logits = (x2d @ w_router).astype(jnp.float32) + b_router.astype(jnp.float32)
    top_logits, top_e = lax.top_k(logits, TOP_K_EXPERTS)
    gates = jax.nn.softmax(top_logits, axis=-1)
    n_slots = N * TOP_K_EXPERTS
    flat_e = top_e.reshape(-1)
    order = jnp.argsort(flat_e)
    sorted_e = flat_e[order]
    counts = jnp.bincount(flat_e, length=N_EXPERTS)
    padded = (counts + MOE_BLOCK - 1) // MOE_BLOCK * MOE_BLOCK
    start = jnp.cumsum(counts) - counts
    pend = jnp.cumsum(padded)
    pstart = pend - padded
    dest = pstart[sorted_e] + (jnp.arange(n_slots) - start[sorted_e])
    n_blocks = (n_slots + N_EXPERTS * (MOE_BLOCK - 1) + MOE_BLOCK - 1) // MOE_BLOCK
    cap = n_blocks * MOE_BLOCK
    buf_tok = jnp.zeros((cap,), jnp.int32).at[dest].set((order // TOP_K_EXPERTS).astype(jnp.int32))
    block_e = jnp.clip(jnp.searchsorted(pend, jnp.arange(n_blocks) * MOE_BLOCK, side='right'), 0, N_EXPERTS - 1)
    xb = x2d[buf_tok].reshape(n_blocks, MOE_BLOCK, D)

    def expert_block(args):
        xe, e = args
        h = xe @ w_gu[e] + b_gu[e]
        gate = jnp.minimum(h[:, :D_FF], SWIGLU_LIMIT)
        up = jnp.clip(h[:, D_FF:], -SWIGLU_LIMIT, SWIGLU_LIMIT)
        act = (up + 1) * gate * jax.nn.sigmoid(SWIGLU_ALPHA * gate)
        return act @ w_dn[e] + b_dn[e]

    yb = lax.map(expert_block, (xb, block_e)).reshape(cap, D)
    y_slots = jnp.zeros((n_slots, D), yb.dtype).at[order].set(yb[dest]).reshape(N, TOP_K_EXPERTS, D)
    return jnp.einsum('nkd,nk->nd', y_slots, gates.astype(yb.dtype))


def block_forward(x, c, conv_prefix, attend, w_mod, b_mod, g_pre_mix, g_post_mix, w_in, conv_w,
                  w_conv_out, w_attn_out, w_mix_out, g_pre_ffn, g_post_ffn,
                  w_router, b_router, w_gu, b_gu, w_dn, b_dn):
    Bn, T, _ = x.shape
    mod = (jax.nn.silu(c) @ w_mod + b_mod)[:, None, :]
    sh1, sc1, ga1, sh2, sc2, ga2 = jnp.split(mod, 6, axis=-1)
    h = rms_norm(x, g_pre_mix) * (1 + sc1) + sh1
    p = h @ w_in
    b_gate = p[..., OFF_CB:OFF_CC]
    c_gate = p[..., OFF_CC:OFF_CX]
    xin = p[..., OFF_CX:OFF_Q]
    q = p[..., OFF_Q:OFF_K].reshape(Bn, T, N_HEADS, HEAD_DIM)
    k = p[..., OFF_K:OFF_V].reshape(Bn, T, N_HEADS, HEAD_DIM)
    v = p[..., OFF_V:OFF_QI].reshape(Bn, T, N_HEADS, HEAD_DIM)
    qi = p[..., OFF_QI:OFF_KI].reshape(Bn, T, N_IDX_HEADS, IDX_DIM)
    kidx = p[..., OFF_KI:OFF_WI]
    wi = p[..., OFF_WI:OFF_GA]
    g_a = jax.nn.sigmoid(p[..., OFF_GA:OFF_GB])
    g_b = jax.nn.sigmoid(p[..., OFF_GB:D_IN_PROJ])
    y_conv, conv_state = short_conv(b_gate, c_gate, xin, conv_prefix, conv_w)
    y_attn = attend(q, k, v, qi, wi, kidx).reshape(Bn, T, D_ATTN)
    merged = g_a * (y_conv @ w_conv_out) + g_b * (y_attn @ w_attn_out)
    x = x + ga1 * rms_norm(merged @ w_mix_out, g_post_mix)
    h2 = rms_norm(x, g_pre_ffn) * (1 + sc2) + sh2
    f = moe_ffn(h2.reshape(Bn * T, D_MODEL), w_router, b_router, w_gu, b_gu, w_dn, b_dn).reshape(Bn, T, D_MODEL)
    x = x + ga2 * rms_norm(f, g_post_ffn)
    return x, k, v, kidx, conv_state


def setup_inputs(seed: int = 0) -> dict:
    key = jax.random.key(seed)
    ks = jax.random.split(key, 27)
    f32 = jnp.float32
    n_pages = PAST_LEN // PAGE_SIZE
    n_phys = (DEC_BATCH * n_pages * 5) // 4

    def nrm(k, shape, scale=1.0):
        return jax.random.normal(k, shape, f32) * scale

    page_table = jax.random.permutation(ks[6], n_phys)[:DEC_BATCH * n_pages].reshape(DEC_BATCH, n_pages).astype(jnp.int32)
    return {
        'x_prompt': nrm(ks[0], (BATCH, SEQ, D_MODEL)),
        'x_sample': nrm(ks[1], (DEC_BATCH, DEC_SEQ, D_MODEL)),
        'cache_k': nrm(ks[2], (DEPTH, n_phys, PAGE_SIZE, N_HEADS, HEAD_DIM)),
        'cache_v': nrm(ks[3], (DEPTH, n_phys, PAGE_SIZE, N_HEADS, HEAD_DIM)),
        'cache_kidx': nrm(ks[4], (DEPTH, n_phys, PAGE_SIZE, IDX_DIM)),
        'state_conv': nrm(ks[5], (DEPTH, DEC_BATCH, CONV_W - 1, D_CONV)),
        'page_table': page_table,
        'c_prompt': nrm(ks[7], (BATCH, D_MODEL)),
        'c_sample': nrm(ks[8], (DEC_BATCH, D_MODEL)),
        'rel_bias': nrm(ks[9], (N_BUCKETS, N_HEADS), 0.5),
        'w_mod': nrm(ks[10], (DEPTH, D_MODEL, 6 * D_MODEL), D_MODEL ** -0.5),
        'b_mod': nrm(ks[11], (DEPTH, 6 * D_MODEL), 0.01),
        'g_pre_mix': 1.0 + nrm(ks[12], (DEPTH, D_MODEL), 0.05),
        'g_post_mix': 1.0 + nrm(ks[13], (DEPTH, D_MODEL), 0.05),
        'w_in': nrm(ks[14], (DEPTH, D_MODEL, D_IN_PROJ), D_MODEL ** -0.5),
        'conv_w': nrm(ks[15], (DEPTH, CONV_W, D_CONV), CONV_W ** -0.5),
        'w_conv_out': nrm(ks[16], (DEPTH, D_CONV, D_MODEL), D_CONV ** -0.5),
        'w_attn_out': nrm(ks[17], (DEPTH, D_ATTN, D_MODEL), D_ATTN ** -0.5),
        'w_mix_out': nrm(ks[18], (DEPTH, D_MODEL, D_MODEL), D_MODEL ** -0.5),
        'g_pre_ffn': 1.0 + nrm(ks[19], (DEPTH, D_MODEL), 0.05),
        'g_post_ffn': 1.0 + nrm(ks[20], (DEPTH, D_MODEL), 0.05),
        'w_router': nrm(ks[21], (DEPTH, D_MODEL, N_EXPERTS), D_MODEL ** -0.5),
        'b_router': nrm(ks[22], (DEPTH, N_EXPERTS), 0.01),
        'w_gu': nrm(ks[23], (DEPTH, N_EXPERTS, D_MODEL, 2 * D_FF), D_MODEL ** -0.5),
        'b_gu': nrm(ks[24], (DEPTH, N_EXPERTS, 2 * D_FF), 0.01),
        'w_dn': nrm(ks[25], (DEPTH, N_EXPERTS, D_FF, D_MODEL), D_FF ** -0.5),
        'b_dn': nrm(ks[26], (DEPTH, N_EXPERTS, D_MODEL), 0.01),
    }


def reference(x_prompt, x_sample, cache_k, cache_v, cache_kidx, state_conv, page_table, c_prompt, c_sample,
              rel_bias, w_mod, b_mod, g_pre_mix, g_post_mix, w_in, conv_w, w_conv_out, w_attn_out, w_mix_out,
              g_pre_ffn, g_post_ffn, w_router, b_router, w_gu, b_gu, w_dn, b_dn):
    xp, xs = x_prompt, x_sample
    kp_l, vp_l, kip_l, cp_l, ks_l, vs_l, kis_l, cs_l = [], [], [], [], [], [], [], []
    for l in range(DEPTH):
        lw = (w_mod[l], b_mod[l], g_pre_mix[l], g_post_mix[l], w_in[l], conv_w[l], w_conv_out[l],
              w_attn_out[l], w_mix_out[l], g_pre_ffn[l], g_post_ffn[l], w_router[l], b_router[l],
              w_gu[l], b_gu[l], w_dn[l], b_dn[l])
        prefix = jnp.zeros((xp.shape[0], CONV_W - 1, D_CONV), xp.dtype)
        attend_p = functools.partial(prompt_attention, rel_bias=rel_bias)
        attend_s = functools.partial(sample_attention, cache_k=cache_k, cache_v=cache_v, cache_kidx=cache_kidx,
                                     page_table=page_table, layer=l, rel_bias=rel_bias)
        xp, kp, vp, kip, cp = block_forward(xp, c_prompt, prefix, attend_p, *lw)
        xs, ks, vs, kis, cs = block_forward(xs, c_sample, state_conv[l], attend_s, *lw)
        kp_l.append(kp); vp_l.append(vp); kip_l.append(kip); cp_l.append(cp)
        ks_l.append(ks); vs_l.append(vs); kis_l.append(kis); cs_l.append(cs)
    k_prompt = jnp.stack(kp_l)
    v_prompt = jnp.stack(vp_l)
    kidx_prompt = jnp.stack(kip_l)
    conv_prompt = jnp.stack(cp_l)
    k_sample = jnp.stack(ks_l)
    v_sample = jnp.stack(vs_l)
    kidx_sample = jnp.stack(kis_l)
    conv_sample = jnp.stack(cs_l)
    return (xp, xs, k_prompt, v_prompt, kidx_prompt, conv_prompt, k_sample, v_sample, kidx_sample, conv_sample)
```

```python
import functools
import math

import jax
import jax.numpy as jnp
from jax import lax
from jax.experimental import pallas as pl
from jax.experimental.pallas import tpu as pltpu

F32 = jnp.float32
BF16 = jnp.bfloat16
I32 = jnp.int32

D_CONV = 512
CONV_W = 3
N_HEADS = 8
HEAD_DIM = 64
D_ATTN = N_HEADS * HEAD_DIM
ATTN_SCALE = HEAD_DIM ** -0.5
N_IDX_HEADS = 8
IDX_DIM = 64
INDEX_SCALE = (N_IDX_HEADS * IDX_DIM) ** -0.5
TOPK_MAX = 256
N_BUCKETS = 32
MAX_DISTANCE = 128
N_EXPERTS = 32
TOP_K_EXPERTS = 4
D_FF = 1024
SWIGLU_LIMIT = 7.0
SWIGLU_ALPHA = 1.702
EPS = 1e-6

LANES = 128
SUBLANES = 8
VMEM_LIMIT_BYTES = 56 * 1024 * 1024

OFF_Q = 3 * D_CONV
OFF_K = OFF_Q + D_ATTN
OFF_V = OFF_K + D_ATTN
OFF_QI = OFF_V + D_ATTN
OFF_KI = OFF_QI + N_IDX_HEADS * IDX_DIM
OFF_WI = OFF_KI + IDX_DIM
OFF_GA = OFF_WI + N_IDX_HEADS

INT_MIN = -2 ** 31
NEG_BIG = -1e30
MOE_ROWS = 512


def _params(*sem):
    return pltpu.CompilerParams(dimension_semantics=sem, vmem_limit_bytes=VMEM_LIMIT_BYTES)


def _const_spec(shape):
    zeros = (0,) * len(shape)
    return pl.BlockSpec(shape, lambda *_: zeros, pipeline_mode=pl.Buffered(1))


def _rms(x, g):
    ms = jnp.mean(x * x, axis=-1, keepdims=True)
    return (x * lax.rsqrt(ms + EPS)) * g


def _sort_key(x):
    bits = pltpu.bitcast(x, I32)
    return jnp.where(bits < 0, bits ^ jnp.int32(0x7FFFFFFF), bits)


def _mod_kernel(c_ref, w_ref, b_ref, o_ref):
    c = c_ref[...]
    s = c * jax.nn.sigmoid(c)
    o_ref[...] = jnp.dot(s.astype(BF16), w_ref[...], preferred_element_type=F32) + b_ref[...]


def _modulation(c, w_mod, b_mod):
    n, d = c.shape
    n_out = w_mod.shape[1]
    bn = 1024
    return pl.pallas_call(
        _mod_kernel,
        grid=(n_out // bn,),
        in_specs=[pl.BlockSpec((n, d), lambda j: (0, 0)),
                  pl.BlockSpec((d, bn), lambda j: (0, j)),
                  pl.BlockSpec((1, bn), lambda j: (0, j))],
        out_specs=pl.BlockSpec((n, bn), lambda j: (0, j)),
        out_shape=jax.ShapeDtypeStruct((n, n_out), F32),
        compiler_params=_params("arbitrary"),
        name="modulation",
    )(c, w_mod, b_mod.reshape(1, n_out))


def _inproj_kernel(x_ref, sc_ref, sh_ref, st_ref, g_ref, wa_ref, wk_ref, wga_ref, wgb_ref, cw_ref, wco_ref,
                   q_ref, k_ref, v_ref, qi_ref, kiwi_ref, mc_ref, gb_ref, u_ref, carry_ref, *, seq_conv):
    x = x_ref[0]
    t = x.shape[0]
    h = _rms(x, g_ref[...]) * (1.0 + sc_ref[0]) + sh_ref[0]
    hb = h.astype(BF16)

    def proj(lo, hi):
        return jnp.dot(hb, wa_ref[:, lo:hi], preferred_element_type=F32)

    b_gate = proj(0, D_CONV)
    u = proj(D_CONV, 2 * D_CONV) * proj(2 * D_CONV, 3 * D_CONV)
    if seq_conv:
        @pl.when(pl.program_id(1) == 0)
        def _():
            carry_ref[0:1, :] = st_ref[0][:, 0:D_CONV]
            carry_ref[1:2, :] = st_ref[0][:, D_CONV:2 * D_CONV]

        cm2 = carry_ref[0:1, :]
        cm1 = carry_ref[1:2, :]
        rows = lax.broadcasted_iota(I32, u.shape, 0)
        prev1 = jnp.where(rows == 0, cm1, pltpu.roll(u, 1, 0))
        prev2 = jnp.where(rows == 0, cm2, jnp.where(rows == 1, cm1, pltpu.roll(u, 2, 0)))
        carry_ref[...] = u[t - 2:t, :]
        u_ref[0] = u[t - SUBLANES:t, :]
    else:
        prev2 = st_ref[0][:, 0:D_CONV]
        prev1 = st_ref[0][:, D_CONV:2 * D_CONV]
        u_ref[0] = u
    cw = cw_ref[...]
    y_conv = b_gate * (cw[0:1, :] * prev2 + cw[1:2, :] * prev1 + cw[2:3, :] * u)
    g_a = jax.nn.sigmoid(jnp.dot(hb, wga_ref[...], preferred_element_type=F32))
    mc_ref[0] = g_a * jnp.dot(y_conv.astype(BF16), wco_ref[...], preferred_element_type=F32)
    gb_ref[0] = jax.nn.sigmoid(jnp.dot(hb, wgb_ref[...], preferred_element_type=F32))
    q_ref[0] = (proj(OFF_Q, OFF_K) * ATTN_SCALE).astype(BF16)
    k_ref[0] = proj(OFF_K, OFF_V)
    v_ref[0] = proj(OFF_V, OFF_QI)
    qi_ref[0] = proj(OFF_QI, OFF_KI).astype(BF16)
    kiwi_ref[0] = jnp.dot(hb, wk_ref[...], preferred_element_type=F32)


def _in_projection(x, sc, sh, st, g, wts, *, seq_conv, rows):
    b, t, d = x.shape
    nblk = t // rows
    tm = 1 if sc.shape[1] == 1 else rows
    ts = 1 if seq_conv else rows
    tu = SUBLANES if seq_conv else rows
    wa, wk, wga, wgb, cw, wco = wts

    def row_spec(width, per_row):
        if per_row:
            return pl.BlockSpec((1, rows, width), lambda i, j: (i, j, 0))
        return pl.BlockSpec((1, 1, width), lambda i, j: (i, 0, 0))

    out_widths = (D_ATTN, D_ATTN, D_ATTN, N_IDX_HEADS * IDX_DIM, LANES, d, d)
    out_dtypes = (BF16, F32, F32, BF16, F32, F32, F32)
    out_shape = [jax.ShapeDtypeStruct((b, t, w), dt) for w, dt in zip(out_widths, out_dtypes)]
    out_specs = [row_spec(w, True) for w in out_widths]
    if seq_conv:
        out_shape.append(jax.ShapeDtypeStruct((b, tu, D_CONV), F32))
        out_specs.append(pl.BlockSpec((1, tu, D_CONV), lambda i, j: (i, 0, 0)))
    else:
        out_shape.append(jax.ShapeDtypeStruct((b, t, D_CONV), F32))
        out_specs.append(row_spec(D_CONV, True))
    return pl.pallas_call(
        functools.partial(_inproj_kernel, seq_conv=seq_conv),
        grid=(b, nblk),
        in_specs=[row_spec(d, True), row_spec(d, tm != 1), row_spec(d, tm != 1),
                  row_spec(2 * D_CONV, ts != 1),
                  _const_spec(g.shape), _const_spec(wa.shape), _const_spec(wk.shape),
                  _const_spec(wga.shape), _const_spec(wgb.shape), _const_spec(cw.shape),
                  _const_spec(wco.shape)],
        out_specs=out_specs,
        out_shape=out_shape,
        scratch_shapes=[pltpu.VMEM((2, D_CONV), F32)],
        compiler_params=_params("arbitrary", "arbitrary"),
        name="in_projection",
    )(x, sc, sh, st, g, wa, wk, wga, wgb, cw, wco)


def _count_rows(keys_ref, nblk, tq, pred):
    def body(kb, acc):
        off = pl.multiple_of(kb * tq, tq)
        blk = keys_ref[:, pl.ds(off, tq)]
        for c in range(tq // LANES):
            acc = acc + jnp.where(pred(blk[:, c * LANES:(c + 1) * LANES]), 1, 0)
        return acc

    acc = lax.fori_loop(0, nblk, body, jnp.zeros((tq, LANES), I32))
    return jnp.sum(acc, axis=1, keepdims=True)


def _kth_largest_key(keys_ref, nblk, tq, k):
    def bit_body(i, kk):
        cand = kk + (jnp.int32(1) << (31 - i))
        cnt = _count_rows(keys_ref, nblk, tq, lambda blk: blk >= cand)
        return jnp.where(cnt >= k, cand, kk)

    return lax.fori_loop(0, 32, bit_body, jnp.full((tq, 1), INT_MIN, I32))


def _pattn_kernel(q_ref, qi_ref, kiwi_ref, kidxt_ref, kt_ref, v_ref, bias_ref, o_ref,
                  keys_ref, m_ref, l_ref, acc_ref, eqc_ref, *, tq, topk):
    qb = pl.program_id(1)
    nkb = qb + 1
    q = q_ref[0]
    qi = qi_ref[0]
    wi = kiwi_ref[0][:, IDX_DIM:IDX_DIM + N_IDX_HEADS]
    row_g = qb * tq + lax.broadcasted_iota(I32, (tq, tq), 0)
    col_l = lax.broadcasted_iota(I32, (tq, tq), 1)

    def score_block(kb, carry):
        off = pl.multiple_of(kb * tq, tq)
        kx = kidxt_ref[0, :, pl.ds(off, tq)]
        acc = jnp.zeros((tq, tq), F32)
        for h in range(N_IDX_HEADS):
            s = jnp.dot(qi[:, h * IDX_DIM:(h + 1) * IDX_DIM], kx, preferred_element_type=F32)
            acc = acc + jnp.maximum(s, 0.0) * wi[:, h:h + 1]
        key = _sort_key(acc * INDEX_SCALE)
        keys_ref[:, pl.ds(off, tq)] = jnp.where(off + col_l <= row_g, key, INT_MIN)
        return carry

    lax.fori_loop(0, nkb, score_block, 0)

    kth = _kth_largest_key(keys_ref, nkb, tq, topk)
    cnt_gt = _count_rows(keys_ref, nkb, tq, lambda blk: blk > kth)
    need = jnp.where(kth == INT_MIN, 0, topk - cnt_gt).astype(F32)

    m_ref[...] = jnp.full(m_ref.shape, NEG_BIG, F32)
    l_ref[...] = jnp.zeros(l_ref.shape, F32)
    acc_ref[...] = jnp.zeros(acc_ref.shape, F32)
    eqc_ref[...] = jnp.zeros(eqc_ref.shape, F32)
    upper = jnp.where(lax.broadcasted_iota(I32, (tq, tq), 0) < col_l, 1.0, 0.0).astype(BF16)

    def attend_block(kb, bias_lo):
        off = pl.multiple_of(kb * tq, tq)
        sk = keys_ref[:, pl.ds(off, tq)]
        eq = sk == kth
        eqf = jnp.where(eq, 1.0, 0.0)
        rank = jnp.dot(eqf.astype(BF16), upper, preferred_element_type=F32) + eqc_ref[...]
        eqc_ref[...] = eqc_ref[...] + jnp.sum(eqf, axis=1, keepdims=True)
        madd = jnp.where(sk > kth, 0.0, jnp.where(eq, jnp.where(rank < need, 0.0, NEG_BIG), NEG_BIG))
        for h in range(N_HEADS):
            hs = slice(h * HEAD_DIM, (h + 1) * HEAD_DIM)
            s = jnp.dot(q[:, hs], kt_ref[0, hs, pl.ds(off, tq)], preferred_element_type=F32) + madd
            if bias_lo is not None:
                s = s + bias_ref[h, :, bias_lo:bias_lo + tq]
            m_old = m_ref[h]
            m_new = jnp.maximum(m_old, jnp.max(s, axis=1, keepdims=True))
            alpha = jnp.exp(m_old - m_new)
            p = jnp.exp(s - m_new)
            l_ref[h] = alpha * l_ref[h] + jnp.sum(p, axis=1, keepdims=True)
            pv = jnp.dot(p.astype(BF16), v_ref[0, pl.ds(off, tq), hs], preferred_element_type=F32)
            acc_ref[h] = alpha * acc_ref[h] + pv
            m_ref[h] = m_new

    def far_block(kb, carry):
        attend_block(kb, None)
        return carry

    lax.fori_loop(0, jnp.maximum(qb - 1, 0), far_block, 0)

    @pl.when(qb >= 1)
    def _():
        attend_block(qb - 1, 0)

    attend_block(qb, tq)
    for h in range(N_HEADS):
        o_ref[0, :, h * HEAD_DIM:(h + 1) * HEAD_DIM] = (acc_ref[h] / l_ref[h]).astype(o_ref.dtype)


def _prompt_attention(q, qi, kiwi, kidxt, kt, v, bias_rel, *, tq):
    b, s, _ = q.shape
    topk = min(TOPK_MAX, s // 4)
    assert tq >= MAX_DISTANCE and s % tq == 0
    blk = lambda w: pl.BlockSpec((1, tq, w), lambda i, j: (i, j, 0))
    per_b = lambda r, c: pl.BlockSpec((1, r, c), lambda i, j: (i, 0, 0), pipeline_mode=pl.Buffered(1))
    return pl.pallas_call(
        functools.partial(_pattn_kernel, tq=tq, topk=topk),
        grid=(b, s // tq),
        in_specs=[blk(D_ATTN), blk(N_IDX_HEADS * IDX_DIM), blk(LANES),
                  per_b(IDX_DIM, s), per_b(D_ATTN, s), per_b(s, D_ATTN), _const_spec(bias_rel.shape)],
        out_specs=blk(D_ATTN),
        out_shape=jax.ShapeDtypeStruct((b, s, D_ATTN), BF16),
        scratch_shapes=[pltpu.VMEM((tq, s), I32),
                        pltpu.VMEM((N_HEADS, tq, 1), F32), pltpu.VMEM((N_HEADS, tq, 1), F32),
                        pltpu.VMEM((N_HEADS, tq, HEAD_DIM), F32), pltpu.VMEM((tq, 1), F32)],
        compiler_params=_params("arbitrary", "arbitrary"),
        name="prompt_attention",
    )(q, qi, kiwi, kidxt, kt, v, bias_rel)


def _sscore_kernel(pt_ref, qi_ref, wi_ref, kn_ref, cache_ref, o_ref, buf, sem, *, n_pages, page):
    b = pl.program_id(0)
    nb = pl.num_programs(0)
    past = n_pages * page

    def page_copy(bb, slot, p):
        return pltpu.make_async_copy(cache_ref.at[pt_ref[bb, p]], buf.at[slot, p], sem.at[slot])

    def fetch(bb, slot):
        for p in range(n_pages):
            page_copy(bb, slot, p).start()

    @pl.when(b == 0)
    def _():
        fetch(0, 0)

    @pl.when(b + 1 < nb)
    def _():
        fetch(b + 1, (b + 1) % 2)

    slot = b % 2
    for p in range(n_pages):
        page_copy(b, slot, p).wait()
    qi = qi_ref[0]
    wi = wi_ref[0]
    kb = buf[slot].reshape(past, IDX_DIM).astype(BF16)
    s = lax.dot_general(qi, kb, (((1,), (1,)), ((), ())), preferred_element_type=F32)
    o_ref[0, :, 0:past] = jnp.sum(jnp.maximum(s, 0.0) * wi, axis=0, keepdims=True) * INDEX_SCALE
    kn = kn_ref[0].astype(BF16).astype(F32)
    sn = jnp.sum(qi.astype(F32) * kn, axis=1, keepdims=True)
    s_new = jnp.sum(jnp.maximum(sn, 0.0) * wi, axis=0, keepdims=True) * INDEX_SCALE
    lane = lax.broadcasted_iota(I32, (1, LANES), 1)
    o_ref[0, :, past:past + LANES] = jnp.where(lane == 0, s_new, -jnp.inf)


def _sample_scores(page_table, qi, wi, kidx_new, cache_kidx):
    nb, n_pages = page_table.shape
    page = cache_kidx.shape[1]
    past = n_pages * page
    grid_spec = pltpu.PrefetchScalarGridSpec(
        num_scalar_prefetch=1,
        grid=(nb,),
        in_specs=[pl.BlockSpec((1, N_IDX_HEADS, IDX_DIM), lambda i, pt: (i, 0, 0)),
                  pl.BlockSpec((1, N_IDX_HEADS, 1), lambda i, pt: (i, 0, 0)),
                  pl.BlockSpec((1, 1, IDX_DIM), lambda i, pt: (i, 0, 0)),
                  pl.BlockSpec(memory_space=pl.ANY)],
        out_specs=pl.BlockSpec((1, 1, past + LANES), lambda i, pt: (i, 0, 0)),
        scratch_shapes=[pltpu.VMEM((2, n_pages, page, IDX_DIM), F32), pltpu.SemaphoreType.DMA((2,))],
    )
    return pl.pallas_call(
        functools.partial(_sscore_kernel, n_pages=n_pages, page=page),
        grid_spec=grid_spec,
        out_shape=jax.ShapeDtypeStruct((nb, 1, past + LANES), F32),
        compiler_params=_params("arbitrary"),
        name="sample_scores",
    )(page_table, qi, wi, kidx_new, cache_kidx)


def _ssel_kernel(sc_ref, idx_ref, keys_ref, incl_ref, *, topk):
    nb, width = sc_ref.shape
    nblk = width // LANES
    keys_ref[...] = _sort_key(sc_ref[...])

    def count(pred):
        def body(c, acc):
            off = pl.multiple_of(c * LANES, LANES)
            return acc + jnp.where(pred(keys_ref[:, pl.ds(off, LANES)]), 1, 0)
        acc = lax.fori_loop(0, nblk, body, jnp.zeros((nb, LANES), I32))
        return jnp.sum(acc, axis=1, keepdims=True)

    def bit_body(i, kk):
        cand = kk + (jnp.int32(1) << (31 - i))
        return jnp.where(count(lambda blk: blk >= cand) >= topk, cand, kk)

    kth = lax.fori_loop(0, 32, bit_body, jnp.full((nb, 1), INT_MIN, I32))
    need = (topk - count(lambda blk: blk > kth)).astype(F32)
    r_i = lax.broadcasted_iota(I32, (LANES, LANES), 0)
    c_i = lax.broadcasted_iota(I32, (LANES, LANES), 1)
    upper = jnp.where(r_i < c_i, 1.0, 0.0).astype(BF16)
    upper_incl = jnp.where(r_i <= c_i, 1.0, 0.0).astype(BF16)

    def prefix_block(c, carry):
        eqc, selc = carry
        off = pl.multiple_of(c * LANES, LANES)
        sk = keys_ref[:, pl.ds(off, LANES)]
        eq = sk == kth
        eqf = jnp.where(eq, 1.0, 0.0)
        rank = jnp.dot(eqf.astype(BF16), upper, preferred_element_type=F32) + eqc
        sel = jnp.where(sk > kth, 1.0, jnp.where(eq, jnp.where(rank < need, 1.0, 0.0), 0.0))
        incl_ref[:, pl.ds(off, LANES)] = jnp.dot(sel.astype(BF16), upper_incl, preferred_element_type=F32) + selc
        return (eqc + jnp.sum(eqf, axis=1, keepdims=True), selc + jnp.sum(sel, axis=1, keepdims=True))

    zero = jnp.zeros((nb, 1), F32)
    lax.fori_loop(0, nblk, prefix_block, (zero, zero))

    slot_f = lax.broadcasted_iota(I32, (topk, LANES), 0).astype(F32)
    lane_b = lax.broadcasted_iota(I32, (topk, LANES), 1)

    def compact_rows(g, out):
        base = pl.multiple_of(g * SUBLANES, SUBLANES)
        for r in range(SUBLANES):
            def body(c, acc):
                off = pl.multiple_of(c * LANES, LANES)
                row = incl_ref[pl.ds(base, SUBLANES), pl.ds(off, LANES)][r:r + 1, :]
                return acc + jnp.where(row <= slot_f, 1, 0)
            acc = lax.fori_loop(0, nblk, body, jnp.zeros((topk, LANES), I32))
            pos = jnp.sum(acc, axis=1, keepdims=True)
            out = jnp.where(lane_b == base + r, pos, out)
        return out

    idx_ref[...] = lax.fori_loop(0, nb // SUBLANES, compact_rows, jnp.zeros((topk, LANES), I32))


def _sample_select(scores, topk):
    nb, width = scores.shape
    assert nb <= LANES
    return pl.pallas_call(
        functools.partial(_ssel_kernel, topk=topk),
        out_shape=jax.ShapeDtypeStruct((topk, LANES), I32),
        scratch_shapes=[pltpu.VMEM((nb, width), I32), pltpu.VMEM((nb, width), F32)],
        compiler_params=pltpu.CompilerParams(vmem_limit_bytes=VMEM_LIMIT_BYTES),
        name="sample_select",
    )(scores)


def _sattn_kernel(pt_ref, idx_ref, q_ref, kn_ref, vn_ref, tab_ref, ck_ref, cv_ref, o_ref,
                  kg, vg, lg_ref, sem, *, topk, page, past):
    b = pl.program_id(0)

    def copies(r):
        pos = jnp.minimum(idx_ref[b * topk + r], past - 1)
        phys = pt_ref[b, pos // page]
        off = pos % page
        return (pltpu.make_async_copy(ck_ref.at[phys, off], kg.at[r], sem.at[0]),
                pltpu.make_async_copy(cv_ref.at[phys, off], vg.at[r], sem.at[1]))

    def start(r, carry):
        ck, cv = copies(r)
        ck.start()
        cv.start()
        return carry

    def wait(r, carry):
        ck, cv = copies(r)
        ck.wait()
        cv.wait()
        return carry

    lax.fori_loop(0, topk, start, 0)
    lax.fori_loop(0, topk, wait, 0)
    q = q_ref[0]

    def logit(r, carry):
        pos = idx_ref[b * topk + r]
        dist = jnp.clip(past - pos, 0, MAX_DISTANCE)
        kr = kg[r].astype(BF16).astype(F32)
        s = jnp.sum(q * kr, axis=1, keepdims=True) + tab_ref[dist][:, 0:1]
        lg_ref[r] = jnp.broadcast_to(jnp.where(pos < past, s, NEG_BIG), (N_HEADS, LANES))
        return carry

    lax.fori_loop(0, topk, logit, 0)
    new_sel = idx_ref[b * topk + topk - 1] >= past
    kn = kn_ref[0].astype(BF16).astype(F32)
    s_new = jnp.sum(q * kn, axis=1, keepdims=True) + tab_ref[0][:, 0:1]
    s_new = jnp.where(new_sel, s_new, NEG_BIG)
    lg = lg_ref[...][:, :, 0:1]
    m = jnp.maximum(jnp.max(lg, axis=0), s_new)
    p = jnp.exp(lg - m[None])
    p_new = jnp.exp(s_new - m)
    denom = jnp.sum(p, axis=0) + p_new
    vsel = vg[...].astype(BF16).astype(F32)
    acc = jnp.sum(p.astype(BF16).astype(F32) * vsel, axis=0)
    acc = acc + p_new.astype(BF16).astype(F32) * vn_ref[0].astype(BF16).astype(F32)
    o_ref[0] = acc / denom


def _sample_attention(page_table, idx_flat, q, k_new, v_new, tab, cache_k, cache_v, *, topk):
    nb, n_pages = page_table.shape
    page = cache_k.shape[1]
    head_blk = pl.BlockSpec((1, N_HEADS, HEAD_DIM), lambda i, pt, ix: (i, 0, 0))
    grid_spec = pltpu.PrefetchScalarGridSpec(
        num_scalar_prefetch=2,
        grid=(nb,),
        in_specs=[head_blk, head_blk, head_blk,
                  pl.BlockSpec(tab.shape, lambda i, pt, ix: (0, 0, 0)),
                  pl.BlockSpec(memory_space=pl.ANY), pl.BlockSpec(memory_space=pl.ANY)],
        out_specs=head_blk,
        scratch_shapes=[pltpu.VMEM((topk, N_HEADS, HEAD_DIM), F32), pltpu.VMEM((topk, N_HEADS, HEAD_DIM), F32),
                        pltpu.VMEM((topk, N_HEADS, LANES), F32), pltpu.SemaphoreType.DMA((2,))],
    )
    return pl.pallas_call(
        functools.partial(_sattn_kernel, topk=topk, page=page, past=n_pages * page),
        grid_spec=grid_spec,
        out_shape=jax.ShapeDtypeStruct((nb, N_HEADS, HEAD_DIM), F32),
        compiler_params=_params("arbitrary"),
        name="sample_attention",
    )(page_table, idx_flat, q, k_new, v_new, tab, cache_k, cache_v)


def _mix_kernel(x_ref, ya_ref, mc_ref, gb_ref, ga1_ref, sc2_ref, sh2_ref, gpost_ref, gpre_ref,
                wao_ref, wmo_ref, wr_ref, br_ref, x1_ref, h2_ref, te_ref, gates_ref):
    x = x_ref[0]
    t, d = x.shape
    attn = jnp.dot(ya_ref[0].astype(BF16), wao_ref[...], preferred_element_type=F32)
    merged = mc_ref[0] + gb_ref[0] * attn
    z = jnp.dot(merged.astype(BF16), wmo_ref[...], preferred_element_type=F32)
    x1 = x + ga1_ref[0] * _rms(z, gpost_ref[...])
    x1_ref[0] = x1
    h2 = _rms(x1, gpre_ref[...]) * (1.0 + sc2_ref[0]) + sh2_ref[0]
    for j in range(d // LANES):
        h2_ref[:, j, :] = h2[:, j * LANES:(j + 1) * LANES]
    lane = lax.broadcasted_iota(I32, (t, LANES), 1)
    logits = jnp.dot(h2.astype(BF16), wr_ref[...], preferred_element_type=F32) + br_ref[...]
    work = jnp.where(lane < N_EXPERTS, logits, -jnp.inf)
    top_l, top_e = [], []
    for _ in range(TOP_K_EXPERTS):
        mk = jnp.max(work, axis=1, keepdims=True)
        ek = jnp.min(jnp.where(work == mk, lane, LANES), axis=1, keepdims=True)
        top_l.append(mk)
        top_e.append(ek)
        work = jnp.where(lane == ek, -jnp.inf, work)
    ex = [jnp.exp(tl - top_l[0]) for tl in top_l]
    denom = ex[0] + ex[1] + ex[2] + ex[3]
    te = jnp.zeros((t, LANES), I32)
    gates = jnp.zeros((t, LANES), F32)
    for k in range(TOP_K_EXPERTS):
        te = jnp.where(lane == k, top_e[k], te)
        gates = jnp.where(lane == k, ex[k] / denom, gates)
    te_ref[...] = te
    gates_ref[...] = gates


def _mix_and_route(x, ya, mc, gb, ga1, sc2, sh2, gpost, gpre, wao, wmo, wr, br, *, rows):
    b, t, d = x.shape
    nblk = t // rows
    per_row = ga1.shape[1] != 1

    def row_spec(width, rowwise=True):
        if rowwise:
            return pl.BlockSpec((1, rows, width), lambda i, j: (i, j, 0))
        return pl.BlockSpec((1, 1, width), lambda i, j: (i, 0, 0))

    flat = lambda w: pl.BlockSpec((rows, w), lambda i, j: (i * nblk + j, 0))
    n = b * t
    return pl.pallas_call(
        _mix_kernel,
        grid=(b, nblk),
        in_specs=[row_spec(d), row_spec(D_ATTN), row_spec(d), row_spec(d),
                  row_spec(d, per_row), row_spec(d, per_row), row_spec(d, per_row),
                  _const_spec(gpost.shape), _const_spec(gpre.shape), _const_spec(wao.shape),
                  _const_spec(wmo.shape), _const_spec(wr.shape), _const_spec(br.shape)],
        out_specs=[row_spec(d),
                   pl.BlockSpec((rows, d // LANES, LANES), lambda i, j: (i * nblk + j, 0, 0)),
                   flat(LANES), flat(LANES)],
        out_shape=[jax.ShapeDtypeStruct((b, t, d), F32),
                   jax.ShapeDtypeStruct((n, d // LANES, LANES), F32),
                   jax.ShapeDtypeStruct((n, LANES), I32),
                   jax.ShapeDtypeStruct((n, LANES), F32)],
        compiler_params=_params("arbitrary", "arbitrary"),
        name="mix_and_route",
    )(x, ya, mc, gb, ga1, sc2, sh2, gpost, gpre, wao, wmo, wr, br)


def _rank_kernel(te_ref, rank_ref, cnt_ref, carry_ref):
    @pl.when(pl.program_id(0) == 0)
    def _():
        carry_ref[...] = jnp.zeros(carry_ref.shape, F32)

    te = te_ref[...]
    t = te.shape[0]
    lane = lax.broadcasted_iota(I32, (t, LANES), 1)
    hits = [lane == te[:, k:k + 1] for k in range(TOP_K_EXPERTS)]
    onehot = jnp.zeros((t, LANES), F32)
    for hit in hits:
        onehot = onehot + jnp.where(hit, 1.0, 0.0)
    lower = jnp.where(lax.broadcasted_iota(I32, (t, t), 0) > lax.broadcasted_iota(I32, (t, t), 1), 1.0, 0.0)
    before = jnp.dot(lower.astype(BF16), onehot.astype(BF16), preferred_element_type=F32) + carry_ref[...]
    rank = jnp.zeros((t, LANES), F32)
    for k, hit in enumerate(hits):
        rk = jnp.sum(jnp.where(hit, before, 0.0), axis=1, keepdims=True)
        rank = jnp.where(lane == k, rk, rank)
    rank_ref[...] = rank.astype(I32)
    carry_ref[...] = carry_ref[...] + jnp.sum(onehot, axis=0, keepdims=True)
    cnt_ref[...] = jnp.broadcast_to(carry_ref[...], cnt_ref.shape)


def _expert_ranks(te, *, rows):
    n = te.shape[0]
    return pl.pallas_call(
        _rank_kernel,
        grid=(n // rows,),
        in_specs=[pl.BlockSpec((rows, LANES), lambda i: (i, 0))],
        out_specs=[pl.BlockSpec((rows, LANES), lambda i: (i, 0)),
                   pl.BlockSpec((SUBLANES, LANES), lambda i: (0, 0))],
        out_shape=[jax.ShapeDtypeStruct((n, LANES), I32), jax.ShapeDtypeStruct((SUBLANES, LANES), F32)],
        scratch_shapes=[pltpu.VMEM((1, LANES), F32)],
        compiler_params=_params("arbitrary"),
        name="expert_ranks",
    )(te)


def _scatter_kernel(dest_ref, h2_ref, init_ref, xs_ref, sem, *, rows):
    del init_ref
    base = pl.program_id(0) * rows * TOP_K_EXPERTS

    def row_copy(n, k):
        return pltpu.make_async_copy(h2_ref.at[n], xs_ref.at[dest_ref[base + n * TOP_K_EXPERTS + k]], sem)

    def start(n, carry):
        for k in range(TOP_K_EXPERTS):
            row_copy(n, k).start()
        return carry

    def wait(n, carry):
        for k in range(TOP_K_EXPERTS):
            row_copy(n, k).wait()
        return carry

    lax.fori_loop(0, rows, start, 0)
    lax.fori_loop(0, rows, wait, 0)


def _scatter_rows(dest_flat, h2, cap, *, rows):
    n, s, _ = h2.shape
    init = jnp.zeros((cap, s, LANES), F32)
    grid_spec = pltpu.PrefetchScalarGridSpec(
        num_scalar_prefetch=1,
        grid=(n // rows,),
        in_specs=[pl.BlockSpec((rows, s, LANES), lambda i, d: (i, 0, 0)),
                  pl.BlockSpec(memory_space=pl.ANY)],
        out_specs=pl.BlockSpec(memory_space=pl.ANY),
        scratch_shapes=[pltpu.SemaphoreType.DMA(())],
    )
    return pl.pallas_call(
        functools.partial(_scatter_kernel, rows=rows),
        grid_spec=grid_spec,
        out_shape=jax.ShapeDtypeStruct((cap, s, LANES), F32),
        input_output_aliases={2: 0},
        compiler_params=_params("arbitrary"),
        name="scatter_rows",
    )(dest_flat, h2, init)


def _expert_kernel(be_ref, nu_ref, xs_ref, wgu_ref, bgu_ref, wdn_ref, bdn_ref, yb_ref):
    del be_ref

    @pl.when(pl.program_id(0) < nu_ref[0])
    def _():
        nseg = xs_ref.shape[1]
        x = jnp.concatenate([xs_ref[:, j, :] for j in range(nseg)], axis=1).astype(BF16)
        h = jnp.dot(x, wgu_ref[0], preferred_element_type=F32) + bgu_ref[0]
        gate = jnp.minimum(h[:, :D_FF], SWIGLU_LIMIT)
        up = jnp.clip(h[:, D_FF:], -SWIGLU_LIMIT, SWIGLU_LIMIT)
        act = (up + 1.0) * gate * jax.nn.sigmoid(SWIGLU_ALPHA * gate)
        y = jnp.dot(act.astype(BF16), wdn_ref[0], preferred_element_type=F32) + bdn_ref[0]
        for j in range(nseg):
            yb_ref[:, j, :] = y[:, j * LANES:(j + 1) * LANES]

    @pl.when(pl.program_id(0) >= nu_ref[0])
    def _():
        yb_ref[...] = jnp.zeros(yb_ref.shape, yb_ref.dtype)


def _expert_ffn(block_e, n_used, xs, w_gu, b_gu, w_dn, b_dn):
    cap, s, _ = xs.shape
    d = s * LANES
    nblk = cap // MOE_ROWS
    rows_map = lambda i, be, nu: (jnp.minimum(i, nu[0] - 1), 0, 0)
    by_expert = lambda i, be, nu: (be[i], 0, 0)
    grid_spec = pltpu.PrefetchScalarGridSpec(
        num_scalar_prefetch=2,
        grid=(nblk,),
        in_specs=[pl.BlockSpec((MOE_ROWS, s, LANES), rows_map),
                  pl.BlockSpec((1, d, 2 * D_FF), by_expert),
                  pl.BlockSpec((1, 1, 2 * D_FF), by_expert),
                  pl.BlockSpec((1, D_FF, d), by_expert),
                  pl.BlockSpec((1, 1, d), by_expert)],
        out_specs=pl.BlockSpec((MOE_ROWS, s, LANES), lambda i, be, nu: (i, 0, 0)),
    )
    return pl.pallas_call(
        _expert_kernel,
        grid_spec=grid_spec,
        out_shape=jax.ShapeDtypeStruct((cap, s, LANES), F32),
        compiler_params=_params("arbitrary"),
        name="expert_ffn",
    )(block_e, n_used, xs, w_gu, b_gu, w_dn, b_dn)


def _combine_kernel(dest_ref, gates_ref, x1_ref, ga2_ref, g_ref, yb_ref, o_ref, buf, sem, *, rows):
    nblk = pl.num_programs(1)
    base = (pl.program_id(0) * nblk + pl.program_id(1)) * rows * TOP_K_EXPERTS

    def row_copy(n, k):
        return pltpu.make_async_copy(yb_ref.at[dest_ref[base + n * TOP_K_EXPERTS + k]], buf.at[k, n], sem)

    def start(n, carry):
        for k in range(TOP_K_EXPERTS):
            row_copy(n, k).start()
        return carry

    def wait(n, carry):
        for k in range(TOP_K_EXPERTS):
            row_copy(n, k).wait()
        return carry

    lax.fori_loop(0, rows, start, 0)
    lax.fori_loop(0, rows, wait, 0)
    gates = gates_ref[...]
    nseg = buf.shape[2]
    f = None
    for k in range(TOP_K_EXPERTS):
        yk = jnp.concatenate([buf[k, :, j, :] for j in range(nseg)], axis=1)
        term = yk * gates[:, k:k + 1]
        f = term if f is None else f + term
    o_ref[0] = x1_ref[0] + ga2_ref[0] * _rms(f, g_ref[...])


def _combine(dest_flat, gates, x1, ga2, g, yb, *, rows):
    b, t, d = x1.shape
    nblk = t // rows
    per_row = ga2.shape[1] != 1
    row_blk = pl.BlockSpec((1, rows, d), lambda i, j, ds: (i, j, 0))
    mod_blk = row_blk if per_row else pl.BlockSpec((1, 1, d), lambda i, j, ds: (i, 0, 0))
    grid_spec = pltpu.PrefetchScalarGridSpec(
        num_scalar_prefetch=1,
        grid=(b, nblk),
        in_specs=[pl.BlockSpec((rows, LANES), lambda i, j, ds: (i * nblk + j, 0)),
                  row_blk, mod_blk,
                  pl.BlockSpec(g.shape, lambda i, j, ds: (0, 0)),
                  pl.BlockSpec(memory_space=pl.ANY)],
        out_specs=row_blk,
        scratch_shapes=[pltpu.VMEM((TOP_K_EXPERTS, rows, d // LANES, LANES), F32),
                        pltpu.SemaphoreType.DMA(())],
    )
    return pl.pallas_call(
        functools.partial(_combine_kernel, rows=rows),
        grid_spec=grid_spec,
        out_shape=jax.ShapeDtypeStruct((b, t, d), F32),
        compiler_params=_params("arbitrary", "arbitrary"),
        name="combine",
    )(dest_flat, gates, x1, ga2, g, yb)


def _moe(x1, h2, te, gates, ga2, g_post, w_gu, b_gu, w_dn, b_dn, *, rank_rows, io_rows):
    n = h2.shape[0]
    n_slots = n * TOP_K_EXPERTS
    rank, counts = _expert_ranks(te, rows=rank_rows)
    counts = counts[0, :N_EXPERTS].astype(I32)
    padded = (counts + MOE_ROWS - 1) // MOE_ROWS * MOE_ROWS
    pend = jnp.cumsum(padded)
    pstart = pend - padded
    nblk = (n_slots + N_EXPERTS * (MOE_ROWS - 1) + MOE_ROWS - 1) // MOE_ROWS
    n_used = (pend[-1] // MOE_ROWS).astype(I32)
    blk_start = jnp.arange(nblk, dtype=I32) * MOE_ROWS
    block_e = jnp.clip(jnp.searchsorted(pend, blk_start, side="right"), 0, N_EXPERTS - 1).astype(I32)
    block_e = jnp.where(jnp.arange(nblk) < n_used, block_e, block_e[jnp.maximum(n_used - 1, 0)])
    top_e = te[:, :TOP_K_EXPERTS]
    dest = (pstart[top_e] + rank[:, :TOP_K_EXPERTS]).reshape(-1).astype(I32)
    xs = _scatter_rows(dest, h2, nblk * MOE_ROWS, rows=io_rows)
    yb = _expert_ffn(block_e, n_used.reshape(1), xs, w_gu, b_gu, w_dn, b_dn)
    return _combine(dest, gates, x1, ga2, g_post, yb, rows=io_rows)


def _t5_bucket(dist):
    n = jnp.maximum(dist, 0)
    max_exact = N_BUCKETS // 2
    nf = jnp.maximum(n, 1).astype(F32)
    large = max_exact + (jnp.log(nf / max_exact) / math.log(MAX_DISTANCE / max_exact)
                         * (N_BUCKETS - max_exact)).astype(I32)
    large = jnp.minimum(large, N_BUCKETS - 1)
    return jnp.where(n < max_exact, n, large)


def _split_in_proj(w_in):
    d = w_in.shape[0]
    wa = w_in[:, :OFF_KI].astype(BF16)
    wk = jnp.pad(w_in[:, OFF_KI:OFF_GA], ((0, 0), (0, LANES - (OFF_GA - OFF_KI)))).astype(BF16)
    wga = w_in[:, OFF_GA:OFF_GA + d].astype(BF16)
    wgb = w_in[:, OFF_GA + d:OFF_GA + 2 * d].astype(BF16)
    return wa, wk, wga, wgb


def kernel(x_prompt, x_sample, cache_k, cache_v, cache_kidx, state_conv, page_table, c_prompt, c_sample, rel_bias, w_mod, b_mod, g_pre_mix, g_post_mix, w_in, conv_w, w_conv_out, w_attn_out, w_mix_out, g_pre_ffn, g_post_ffn, w_router, b_router, w_gu, b_gu, w_dn, b_dn):
    depth = w_mod.shape[0]
    bp, seq, d = x_prompt.shape
    nb, dec_seq, _ = x_sample.shape
    assert dec_seq == 1
    page = cache_k.shape[2]
    past = page_table.shape[1] * page
    tq = 256
    rows_p = 256
    s_topk = min(TOPK_MAX, (past + dec_seq) // 4)

    dists = jnp.arange(2 * tq + 1, dtype=I32)
    tab = rel_bias.astype(F32)[_t5_bucket(dists)]
    far = tab[MAX_DISTANCE]
    rel = (tab - far[None]).T
    ii = jnp.arange(tq, dtype=I32)[:, None]
    jj = jnp.arange(2 * tq, dtype=I32)[None, :]
    bias_rel = rel[:, jnp.clip(ii - jj + tq, 0, 2 * tq)]
    tab_s = jnp.broadcast_to(tab[:MAX_DISTANCE + 1, :, None], (MAX_DISTANCE + 1, N_HEADS, LANES))

    xp = x_prompt
    xs_rows = x_sample.reshape(1, nb, d)
    c_all = jnp.concatenate([c_prompt, c_sample], axis=0)
    n_c = c_all.shape[0]
    c_all = jnp.pad(c_all, ((0, -n_c % SUBLANES), (0, 0)))
    outs = [[] for _ in range(8)]
    for l in range(depth):
        mod = _modulation(c_all, w_mod[l].astype(BF16), b_mod[l])
        mod_p = [m[:, None, :] for m in jnp.split(mod[:bp], 6, axis=-1)]
        mod_s = [m[None] for m in jnp.split(mod[bp:bp + nb], 6, axis=-1)]
        wa, wk, wga, wgb = _split_in_proj(w_in[l])
        wts = (wa, wk, wga, wgb, conv_w[l], w_conv_out[l].astype(BF16))
        g1 = g_pre_mix[l].reshape(1, d)
        wao = w_attn_out[l].astype(BF16)
        wmo = w_mix_out[l].astype(BF16)
        wr = jnp.pad(w_router[l], ((0, 0), (0, LANES - N_EXPERTS))).astype(BF16)
        br = jnp.pad(b_router[l], (0, LANES - N_EXPERTS)).reshape(1, LANES)
        gpost = g_post_mix[l].reshape(1, d)
        gpre2 = g_pre_ffn[l].reshape(1, d)
        gpost2 = g_post_ffn[l].reshape(1, d)
        wgu = w_gu[l].astype(BF16)
        bgu = b_gu[l].reshape(N_EXPERTS, 1, 2 * D_FF)
        wdn = w_dn[l].astype(BF16)
        bdn = b_dn[l].reshape(N_EXPERTS, 1, d)

        st = state_conv[l].reshape(1, nb, 2 * D_CONV)
        q, k, v, qi, kiwi, mc, gb, u = _in_projection(
            xs_rows, mod_s[1], mod_s[0], st, g1, wts, seq_conv=False, rows=nb)
        kidx = kiwi[0, :, :IDX_DIM]
        wi = kiwi[0, :, IDX_DIM:IDX_DIM + N_IDX_HEADS]
        scores = _sample_scores(page_table, qi.reshape(nb, N_IDX_HEADS, IDX_DIM), wi[:, :, None],
                                kidx[:, None, :], cache_kidx[l])
        idx = _sample_select(scores.reshape(nb, past + LANES), s_topk)
        idx_flat = idx.T[:nb].reshape(-1)
        k3 = k.reshape(nb, N_HEADS, HEAD_DIM)
        v3 = v.reshape(nb, N_HEADS, HEAD_DIM)
        ya = _sample_attention(page_table, idx_flat, q.astype(F32).reshape(nb, N_HEADS, HEAD_DIM), k3, v3,
                               tab_s, cache_k[l], cache_v[l], topk=s_topk)
        x1, h2, te, gates = _mix_and_route(xs_rows, ya.reshape(1, nb, D_ATTN), mc, gb, mod_s[2], mod_s[4],
                                           mod_s[3], gpost, gpre2, wao, wmo, wr, br, rows=nb)
        xs_rows = _moe(x1, h2, te, gates, mod_s[5], gpost2, wgu, bgu, wdn, bdn, rank_rows=nb, io_rows=nb)
        outs[4].append(k.reshape(nb, dec_seq, N_HEADS, HEAD_DIM))
        outs[5].append(v.reshape(nb, dec_seq, N_HEADS, HEAD_DIM))
        outs[6].append(kidx[:, None, :])
        outs[7].append(jnp.stack([state_conv[l][:, 1, :], u[0]], axis=1))

        prefix = jnp.zeros((bp, 1, 2 * D_CONV), F32)
        q, k, v, qi, kiwi, mc, gb, u_tail = _in_projection(
            xp, mod_p[1], mod_p[0], prefix, g1, wts, seq_conv=True, rows=rows_p)
        kidx = kiwi[..., :IDX_DIM]
        ya = _prompt_attention(q, qi, kiwi, jnp.swapaxes(kidx, 1, 2).astype(BF16),
                               jnp.swapaxes(k, 1, 2).astype(BF16), v.astype(BF16), bias_rel, tq=tq)
        x1, h2, te, gates = _mix_and_route(xp, ya, mc, gb, mod_p[2], mod_p[4], mod_p[3], gpost, gpre2,
                                           wao, wmo, wr, br, rows=rows_p)
        xp = _moe(x1, h2, te, gates, mod_p[5], gpost2, wgu, bgu, wdn, bdn, rank_rows=512, io_rows=rows_p)
        outs[0].append(k.reshape(bp, seq, N_HEADS, HEAD_DIM))
        outs[1].append(v.reshape(bp, seq, N_HEADS, HEAD_DIM))
        outs[2].append(kidx)
        outs[3].append(u_tail[:, SUBLANES - (CONV_W - 1):, :])
    return (xp, xs_rows.reshape(nb, dec_seq, d)) + tuple(jnp.stack(o) for o in outs)
```

```python
import functools
import math

import jax
import jax.numpy as jnp
from jax import lax
from jax.experimental import pallas as pl
from jax.experimental.pallas import tpu as pltpu

F32 = jnp.float32
BF16 = jnp.bfloat16
I32 = jnp.int32

D_CONV = 512
CONV_W = 3
N_HEADS = 8
HEAD_DIM = 64
D_ATTN = N_HEADS * HEAD_DIM
ATTN_SCALE = HEAD_DIM ** -0.5
N_IDX_HEADS = 8
IDX_DIM = 64
INDEX_SCALE = (N_IDX_HEADS * IDX_DIM) ** -0.5
TOPK_MAX = 256
N_BUCKETS = 32
MAX_DISTANCE = 128
N_EXPERTS = 32
TOP_K_EXPERTS = 4
D_FF = 1024
SWIGLU_LIMIT = 7.0
SWIGLU_ALPHA = 1.702
EPS = 1e-6

LANES = 128
SUBLANES = 8
VMEM_LIMIT_BYTES = 56 * 1024 * 1024

OFF_Q = 3 * D_CONV
OFF_K = OFF_Q + D_ATTN
OFF_V = OFF_K + D_ATTN
OFF_QI = OFF_V + D_ATTN
OFF_KI = OFF_QI + N_IDX_HEADS * IDX_DIM
OFF_WI = OFF_KI + IDX_DIM
OFF_GA = OFF_WI + N_IDX_HEADS

INT_MIN = -2 ** 31
NEG_BIG = -1e30
MOE_ROWS = 512


def _params(*sem):
    return pltpu.CompilerParams(dimension_semantics=sem, vmem_limit_bytes=VMEM_LIMIT_BYTES)


def _const_spec(shape):
    zeros = (0,) * len(shape)
    return pl.BlockSpec(shape, lambda *_: zeros, pipeline_mode=pl.Buffered(1))


def _rms(x, g):
    ms = jnp.mean(x * x, axis=-1, keepdims=True)
    return (x * lax.rsqrt(ms + EPS)) * g


def _sort_key(x):
    bits = pltpu.bitcast(x, I32)
    return jnp.where(bits < 0, bits ^ jnp.int32(0x7FFFFFFF), bits)


def _mod_kernel(c_ref, w_ref, b_ref, o_ref):
    c = c_ref[...]
    s = c * jax.nn.sigmoid(c)
    o_ref[...] = jnp.dot(s.astype(BF16), w_ref[...], preferred_element_type=F32) + b_ref[...]


def _modulation(c, w_mod, b_mod):
    n, d = c.shape
    n_out = w_mod.shape[1]
    bn = 1024
    return pl.pallas_call(
        _mod_kernel,
        grid=(n_out // bn,),
        in_specs=[pl.BlockSpec((n, d), lambda j: (0, 0)),
                  pl.BlockSpec((d, bn), lambda j: (0, j)),
                  pl.BlockSpec((1, bn), lambda j: (0, j))],
        out_specs=pl.BlockSpec((n, bn), lambda j: (0, j)),
        out_shape=jax.ShapeDtypeStruct((n, n_out), F32),
        compiler_params=_params("arbitrary"),
        name="modulation",
    )(c, w_mod, b_mod.reshape(1, n_out))


def _inproj_kernel(x_ref, sc_ref, sh_ref, st_ref, g_ref, wa_ref, wk_ref, wga_ref, wgb_ref, cw_ref, wco_ref,
                   q_ref, k_ref, v_ref, qi_ref, kiwi_ref, mc_ref, gb_ref, u_ref, carry_ref, *, seq_conv):
    x = x_ref[0]
    t = x.shape[0]
    h = _rms(x, g_ref[...]) * (1.0 + sc_ref[0]) + sh_ref[0]
    hb = h.astype(BF16)

    def proj(lo, hi):
        return jnp.dot(hb, wa_ref[:, lo:hi], preferred_element_type=F32)

    b_gate = proj(0, D_CONV)
    u = proj(D_CONV, 2 * D_CONV) * proj(2 * D_CONV, 3 * D_CONV)
    if seq_conv:
        @pl.when(pl.program_id(1) == 0)
        def _():
            carry_ref[0:1, :] = st_ref[0][:, 0:D_CONV]
            carry_ref[1:2, :] = st_ref[0][:, D_CONV:2 * D_CONV]

        cm2 = carry_ref[0:1, :]
        cm1 = carry_ref[1:2, :]
        rows = lax.broadcasted_iota(I32, u.shape, 0)
        prev1 = jnp.where(rows == 0, cm1, pltpu.roll(u, 1, 0))
        prev2 = jnp.where(rows == 0, cm2, jnp.where(rows == 1, cm1, pltpu.roll(u, 2, 0)))
        carry_ref[...] = u[t - 2:t, :]
        u_ref[0] = u[t - SUBLANES:t, :]
    else:
        prev2 = st_ref[0][:, 0:D_CONV]
        prev1 = st_ref[0][:, D_CONV:2 * D_CONV]
        u_ref[0] = u
    cw = cw_ref[...]
    y_conv = b_gate * (cw[0:1, :] * prev2 + cw[1:2, :] * prev1 + cw[2:3, :] * u)
    g_a = jax.nn.sigmoid(jnp.dot(hb, wga_ref[...], preferred_element_type=F32))
    mc_ref[0] = g_a * jnp.dot(y_conv.astype(BF16), wco_ref[...], preferred_element_type=F32)
    gb_ref[0] = jax.nn.sigmoid(jnp.dot(hb, wgb_ref[...], preferred_element_type=F32))
    q_ref[0] = (proj(OFF_Q, OFF_K) * ATTN_SCALE).astype(BF16)
    k_ref[0] = proj(OFF_K, OFF_V)
    v_ref[0] = proj(OFF_V, OFF_QI)
    qi_ref[0] = proj(OFF_QI, OFF_KI).astype(BF16)
    kiwi_ref[0] = jnp.dot(hb, wk_ref[...], preferred_element_type=F32)


def _in_projection(x, sc, sh, st, g, wts, *, seq_conv, rows):
    b, t, d = x.shape
    nblk = t // rows
    tm = 1 if sc.shape[1] == 1 else rows
    ts = 1 if seq_conv else rows
    tu = SUBLANES if seq_conv else rows
    wa, wk, wga, wgb, cw, wco = wts

    def row_spec(width, per_row):
        if per_row:
            return pl.BlockSpec((1, rows, width), lambda i, j: (i, j, 0))
        return pl.BlockSpec((1, 1, width), lambda i, j: (i, 0, 0))

    out_widths = (D_ATTN, D_ATTN, D_ATTN, N_IDX_HEADS * IDX_DIM, LANES, d, d)
    out_dtypes = (BF16, F32, F32, BF16, F32, F32, F32)
    out_shape = [jax.ShapeDtypeStruct((b, t, w), dt) for w, dt in zip(out_widths, out_dtypes)]
    out_specs = [row_spec(w, True) for w in out_widths]
    if seq_conv:
        out_shape.append(jax.ShapeDtypeStruct((b, tu, D_CONV), F32))
        out_specs.append(pl.BlockSpec((1, tu, D_CONV), lambda i, j: (i, 0, 0)))
    else:
        out_shape.append(jax.ShapeDtypeStruct((b, t, D_CONV), F32))
        out_specs.append(row_spec(D_CONV, True))
    return pl.pallas_call(
        functools.partial(_inproj_kernel, seq_conv=seq_conv),
        grid=(b, nblk),
        in_specs=[row_spec(d, True), row_spec(d, tm != 1), row_spec(d, tm != 1),
                  row_spec(2 * D_CONV, ts != 1),
                  _const_spec(g.shape), _const_spec(wa.shape), _const_spec(wk.shape),
                  _const_spec(wga.shape), _const_spec(wgb.shape), _const_spec(cw.shape),
                  _const_spec(wco.shape)],
        out_specs=out_specs,
        out_shape=out_shape,
        scratch_shapes=[pltpu.VMEM((2, D_CONV), F32)],
        compiler_params=_params("arbitrary", "arbitrary"),
        name="in_projection",
    )(x, sc, sh, st, g, wa, wk, wga, wgb, cw, wco)


ROW_TILE = 128
ATT_ROWS = 256


def _lane_rep(col, width=LANES):
    return jnp.broadcast_to(col, (col.shape[0], width))


def _tile_lanes(x, n):
    return x if n == 1 else jnp.concatenate([x] * n, axis=1)


def _count_rows(keys_ref, nblk, tq, tk, pred_of_tile):
    out = []
    for r in range(tq // ROW_TILE):
        rows = slice(r * ROW_TILE, (r + 1) * ROW_TILE)
        pred = pred_of_tile(r)

        def body(kb, acc, rows=rows, pred=pred):
            off = pl.multiple_of(kb * tk, tk)
            for c in range(tk // LANES):
                blk = keys_ref[rows, pl.ds(off + c * LANES, LANES)]
                acc = acc + jnp.where(pred(blk), 1.0, 0.0)
            return acc

        out.append(lax.fori_loop(0, nblk, body, jnp.zeros((ROW_TILE, LANES), F32)))
    assert keys_ref.shape[1] // LANES < 256
    partial = jnp.concatenate(out, axis=0).astype(BF16)
    return jnp.dot(partial, jnp.ones((LANES, LANES), BF16), preferred_element_type=F32).astype(I32)


def _kth_largest_key(keys_ref, nblk, tq, tk, k):
    def bit_body(i, kk):
        cand = kk + (jnp.int32(1) << (31 - i))

        def pred_of_tile(r):
            cb = cand[r * ROW_TILE:(r + 1) * ROW_TILE]
            return lambda blk: blk >= cb

        cnt = _count_rows(keys_ref, nblk, tq, tk, pred_of_tile)
        return jnp.where(cnt >= k, cand, kk)

    return lax.fori_loop(0, 32, bit_body, jnp.full((tq, LANES), INT_MIN, I32))


def _pattn_kernel(q_ref, qi_ref, kiwi_ref, kidxt_ref, kt_ref, v_ref, bias_ref, upper_ref, o_ref,
                  keys_ref, wb_ref, madd_ref, m_ref, l_ref, acc_ref, eqc_ref, *, tq, tk, topk):
    qb = pl.program_id(1)
    kd = (qb * tq) // tk
    nkb = kd + 1
    reps = tk // LANES
    wi = kiwi_ref[0][:, IDX_DIM:IDX_DIM + N_IDX_HEADS]
    for h in range(N_IDX_HEADS):
        wb_ref[h] = _lane_rep(wi[:, h:h + 1])

    col_w = 2 * LANES
    row_i = lax.broadcasted_iota(I32, (ROW_TILE, col_w), 0)
    col_i = lax.broadcasted_iota(I32, (ROW_TILE, col_w), 1)

    def score_block(kb, carry):
        off = pl.multiple_of(kb * tk, tk)
        for r in range(tq // ROW_TILE):
            rows = slice(r * ROW_TILE, (r + 1) * ROW_TILE)
            for c in range(tk // col_w):
                kx = kidxt_ref[0, :, pl.ds(off + c * col_w, col_w)]
                acc = jnp.zeros((ROW_TILE, col_w), F32)
                for h in range(N_IDX_HEADS):
                    s = jnp.dot(qi_ref[0, h, rows, :], kx, preferred_element_type=F32)
                    acc = acc + jnp.maximum(s, 0.0) * _tile_lanes(wb_ref[h, rows, :], col_w // LANES)
                key = _sort_key(acc * INDEX_SCALE)
                valid = (off + c * col_w + col_i) <= (qb * tq + r * ROW_TILE + row_i)
                keys_ref[rows, pl.ds(off + c * col_w, col_w)] = jnp.where(valid, key, INT_MIN)
        return carry

    lax.fori_loop(0, nkb, score_block, 0)

    kth = _kth_largest_key(keys_ref, nkb, tq, tk, topk)

    def gt_of_tile(r):
        kb_ = kth[r * ROW_TILE:(r + 1) * ROW_TILE]
        return lambda blk: blk > kb_

    cnt_gt = _count_rows(keys_ref, nkb, tq, tk, gt_of_tile)
    need = jnp.where(kth == INT_MIN, 0, topk - cnt_gt).astype(F32)
    kth_w = _tile_lanes(kth, reps)
    need_w = _tile_lanes(need, reps)

    m_ref[...] = jnp.full(m_ref.shape, NEG_BIG, F32)
    l_ref[...] = jnp.zeros(l_ref.shape, F32)
    acc_ref[...] = jnp.zeros(acc_ref.shape, F32)
    eqc_ref[...] = jnp.zeros(eqc_ref.shape, F32)

    def near_bias(h, kb, r):
        nrow = min(ATT_ROWS, LANES)
        strips = []
        for r0 in range(r * ATT_ROWS, (r + 1) * ATT_ROWS, nrow):
            i, sub = divmod(r0, LANES)
            t0 = bias_ref[h, 0, sub:sub + nrow, :]
            t1 = bias_ref[h, 1, sub:sub + nrow, :]
            tiles = []
            for j in range(reps):
                d = qb * (tq // LANES) + i - kb * reps - j
                tiles.append(jnp.where(d == 0, t0, jnp.where(d == 1, t1, 0.0)))
            strips.append(jnp.concatenate(tiles, axis=1))
        return strips[0] if len(strips) == 1 else jnp.concatenate(strips, axis=0)

    def attend_rows(off, rows, bias_of_head):
        madd = madd_ref[rows, :]
        stats = [(m_ref[h, rows, :], l_ref[h, rows, :], acc_ref[h, rows, :]) for h in range(N_HEADS)]
        new_stats = []
        for h, (m_old, l_old, acc_old) in enumerate(stats):
            hs = slice(h * HEAD_DIM, (h + 1) * HEAD_DIM)
            s = jnp.dot(q_ref[0, h, rows, :], kt_ref[0, hs, pl.ds(off, tk)], preferred_element_type=F32) + madd
            if bias_of_head is not None:
                s = s + bias_of_head(h)
            m_new = jnp.maximum(m_old, _lane_rep(jnp.max(s, axis=1, keepdims=True)))
            alpha = jnp.exp(m_old - m_new)
            p = jnp.exp(s - _tile_lanes(m_new, reps))
            l_new = alpha * l_old + _lane_rep(jnp.sum(p, axis=1, keepdims=True))
            pv = jnp.dot(p.astype(BF16), v_ref[0, pl.ds(off, tk), hs], preferred_element_type=F32)
            new_stats.append((m_new, l_new, alpha[:, :HEAD_DIM] * acc_old + pv))
        for h, (m_new, l_new, acc_new) in enumerate(new_stats):
            m_ref[h, rows, :] = m_new
            l_ref[h, rows, :] = l_new
            acc_ref[h, rows, :] = acc_new

    def attend_block(kb, near):
        off = pl.multiple_of(kb * tk, tk)
        sk = keys_ref[:, pl.ds(off, tk)]
        eq = sk == kth_w
        eqf = jnp.where(eq, 1.0, 0.0)
        rank = jnp.dot(eqf.astype(BF16), upper_ref[...], preferred_element_type=F32)
        rank = rank + _tile_lanes(eqc_ref[...], reps)
        eqc_ref[...] = eqc_ref[...] + _lane_rep(jnp.sum(eqf, axis=1, keepdims=True))
        madd_ref[...] = jnp.where(sk > kth_w, 0.0, jnp.where(eq, jnp.where(rank < need_w, 0.0, NEG_BIG), NEG_BIG))
        for r in range(tq // ATT_ROWS):
            bias_of_head = (lambda h, r=r: near_bias(h, kb, r)) if near else None
            attend_rows(off, slice(r * ATT_ROWS, (r + 1) * ATT_ROWS), bias_of_head)

    def far_block(kb, carry):
        attend_block(kb, False)
        return carry

    lax.fori_loop(0, jnp.maximum(kd - 1, 0), far_block, 0)

    @pl.when(kd >= 1)
    def _():
        attend_block(kd - 1, True)

    attend_block(kd, True)
    for h in range(N_HEADS):
        out_h = acc_ref[h] / l_ref[h][:, :HEAD_DIM]
        o_ref[0, :, h * HEAD_DIM:(h + 1) * HEAD_DIM] = out_h.astype(o_ref.dtype)


def _prompt_attention(q, qi, kiwi, kidxt, kt, v, bias_tiles, *, tq, tk):
    b, _, s, _ = q.shape
    topk = min(TOPK_MAX, s // 4)
    assert tk % (2 * LANES) == 0 and tk % tq == 0 and tq % ROW_TILE == 0 and s % tk == 0
    assert LANES >= MAX_DISTANCE and tk >= 2 * LANES
    upper = jnp.triu(jnp.ones((tk, tk), BF16), 1)
    heads = lambda: pl.BlockSpec((1, N_HEADS, tq, HEAD_DIM), lambda i, j: (i, 0, j, 0))
    per_b = lambda r, c: pl.BlockSpec((1, r, c), lambda i, j: (i, 0, 0), pipeline_mode=pl.Buffered(1))
    stat = lambda w: pltpu.VMEM((N_HEADS, tq, w), F32)
    return pl.pallas_call(
        functools.partial(_pattn_kernel, tq=tq, tk=tk, topk=topk),
        grid=(b, s // tq),
        in_specs=[heads(), heads(), pl.BlockSpec((1, tq, LANES), lambda i, j: (i, j, 0)),
                  per_b(IDX_DIM, s), per_b(D_ATTN, s), per_b(s, D_ATTN),
                  _const_spec(bias_tiles.shape), _const_spec(upper.shape)],
        out_specs=pl.BlockSpec((1, tq, D_ATTN), lambda i, j: (i, j, 0)),
        out_shape=jax.ShapeDtypeStruct((b, s, D_ATTN), BF16),
        scratch_shapes=[pltpu.VMEM((tq, s), I32), stat(LANES), pltpu.VMEM((tq, tk), F32),
                        stat(LANES), stat(LANES), stat(HEAD_DIM), pltpu.VMEM((tq, LANES), F32)],
        compiler_params=_params("arbitrary", "arbitrary"),
        name="prompt_attention",
    )(q, qi, kiwi, kidxt, kt, v, bias_tiles, upper)


def _sscore_kernel(pt_ref, qi_ref, wi_ref, kn_ref, cache_ref, o_ref, buf, sem, *, layer, n_pages, page):
    b = pl.program_id(0)
    nb = pl.num_programs(0)
    past = n_pages * page

    def page_copy(bb, slot, p):
        return pltpu.make_async_copy(cache_ref.at[layer, pt_ref[bb, p]], buf.at[slot, p], sem.at[slot])

    def fetch(bb, slot):
        for p in range(n_pages):
            page_copy(bb, slot, p).start()

    @pl.when(b == 0)
    def _():
        fetch(0, 0)

    @pl.when(b + 1 < nb)
    def _():
        fetch(b + 1, (b + 1) % 2)

    slot = b % 2
    for p in range(n_pages):
        page_copy(b, slot, p).wait()
    qi = qi_ref[0]
    wi = wi_ref[0]
    group = 8
    for g in range(n_pages // group):
        kx = jnp.concatenate([buf[slot, g * group + j] for j in range(group)], axis=1).astype(BF16)
        s = jnp.dot(qi, kx, preferred_element_type=F32)
        sc = jnp.sum(jnp.maximum(s, 0.0) * wi, axis=0, keepdims=True) * INDEX_SCALE
        o_ref[0, :, g * group * page:(g + 1) * group * page] = sc
    kn = kn_ref[0].astype(BF16).astype(F32)
    sn = jnp.sum(qi.astype(F32) * kn, axis=1, keepdims=True)
    s_new = jnp.sum(jnp.maximum(sn, 0.0) * wi, axis=0, keepdims=True) * INDEX_SCALE
    lane = lax.broadcasted_iota(I32, (1, LANES), 1)
    o_ref[0, :, past:past + LANES] = jnp.where(lane == 0, s_new, -jnp.inf)


def _sample_scores(page_table, qi, wi, kidx_new, cache_kidx_t, layer):
    nb, n_pages = page_table.shape
    page = cache_kidx_t.shape[3]
    past = n_pages * page
    assert n_pages % 8 == 0 and page == LANES
    grid_spec = pltpu.PrefetchScalarGridSpec(
        num_scalar_prefetch=1,
        grid=(nb,),
        in_specs=[pl.BlockSpec((1, N_IDX_HEADS, IDX_DIM), lambda i, pt: (i, 0, 0)),
                  pl.BlockSpec((1, N_IDX_HEADS, 1), lambda i, pt: (i, 0, 0)),
                  pl.BlockSpec((1, 1, IDX_DIM), lambda i, pt: (i, 0, 0)),
                  pl.BlockSpec(memory_space=pl.ANY)],
        out_specs=pl.BlockSpec((1, 1, past + LANES), lambda i, pt: (i, 0, 0)),
        scratch_shapes=[pltpu.VMEM((2, n_pages, IDX_DIM, page), F32), pltpu.SemaphoreType.DMA((2,))],
    )
    return pl.pallas_call(
        functools.partial(_sscore_kernel, layer=layer, n_pages=n_pages, page=page),
        grid_spec=grid_spec,
        out_shape=jax.ShapeDtypeStruct((nb, 1, past + LANES), F32),
        compiler_params=_params("arbitrary"),
        name="sample_scores",
    )(page_table, qi, wi, kidx_new, cache_kidx_t)


def _ssel_kernel(sc_ref, madd_ref, keys_ref, *, topk):
    nb, width = sc_ref.shape
    nblk = width // LANES
    keys_ref[...] = _sort_key(sc_ref[...])

    def count(pred):
        def body(c, acc):
            off = pl.multiple_of(c * LANES, LANES)
            return acc + jnp.where(pred(keys_ref[:, pl.ds(off, LANES)]), 1, 0)
        acc = lax.fori_loop(0, nblk, body, jnp.zeros((nb, LANES), I32))
        return jnp.sum(acc, axis=1, keepdims=True)

    def bit_body(i, kk):
        cand = kk + (jnp.int32(1) << (31 - i))
        return jnp.where(count(lambda blk: blk >= cand) >= topk, cand, kk)

    kth = lax.fori_loop(0, 32, bit_body, jnp.full((nb, 1), INT_MIN, I32))
    need = (topk - count(lambda blk: blk > kth)).astype(F32)
    r_i = lax.broadcasted_iota(I32, (LANES, LANES), 0)
    c_i = lax.broadcasted_iota(I32, (LANES, LANES), 1)
    upper = jnp.where(r_i < c_i, 1.0, 0.0).astype(BF16)

    def mask_block(c, eqc):
        off = pl.multiple_of(c * LANES, LANES)
        sk = keys_ref[:, pl.ds(off, LANES)]
        eq = sk == kth
        eqf = jnp.where(eq, 1.0, 0.0)
        rank = jnp.dot(eqf.astype(BF16), upper, preferred_element_type=F32) + eqc
        madd_ref[:, pl.ds(off, LANES)] = jnp.where(
            sk > kth, 0.0, jnp.where(eq, jnp.where(rank < need, 0.0, NEG_BIG), NEG_BIG))
        return eqc + jnp.sum(eqf, axis=1, keepdims=True)

    lax.fori_loop(0, nblk, mask_block, jnp.zeros((nb, 1), F32))


def _sample_select(scores, topk):
    nb, width = scores.shape
    return pl.pallas_call(
        functools.partial(_ssel_kernel, topk=topk),
        out_shape=jax.ShapeDtypeStruct((nb, width), F32),
        scratch_shapes=[pltpu.VMEM((nb, width), I32)],
        compiler_params=pltpu.CompilerParams(vmem_limit_bytes=VMEM_LIMIT_BYTES),
        name="sample_select",
    )(scores)


def _sattn_kernel(pt_ref, q_ref, qt_ref, kn_ref, vnt_ref, madd_ref, bias_ref, ck_ref, cv_ref, o_ref,
                  kbuf, vbuf, qb_ref, acc_ref, m_ref, l_ref, sem, *, layer, cp, page, past):
    b = pl.program_id(0)
    c = pl.program_id(1)
    nc = pl.num_programs(1)
    step = b * nc + c
    n_steps = pl.num_programs(0) * nc

    def page_copies(bb, cc, slot, j):
        phys = pt_ref[bb, cc * cp + j]
        return (pltpu.make_async_copy(ck_ref.at[layer, phys], kbuf.at[slot, j], sem.at[0, slot]),
                pltpu.make_async_copy(cv_ref.at[layer, phys], vbuf.at[slot, j], sem.at[1, slot]))

    def fetch(bb, cc, slot):
        for j in range(cp):
            ck, cv = page_copies(bb, cc, slot, j)
            ck.start()
            cv.start()

    @pl.when(step == 0)
    def _():
        fetch(0, 0, 0)

    nxt = step + 1

    @pl.when(nxt < n_steps)
    def _():
        fetch(nxt // nc, nxt % nc, nxt % 2)

    slot = step % 2
    for j in range(cp):
        ck, cv = page_copies(b, c, slot, j)
        ck.wait()
        cv.wait()

    @pl.when(c == 0)
    def _():
        m_ref[...] = jnp.full(m_ref.shape, NEG_BIG, F32)
        l_ref[...] = jnp.zeros(l_ref.shape, F32)
        acc_ref[...] = jnp.zeros(acc_ref.shape, F32)
        qt = qt_ref[0]
        for h in range(N_HEADS):
            qb_ref[h] = _lane_rep(qt[:, h:h + 1], page)

    pages = []
    for j in range(cp):
        rows = [jnp.sum(kbuf[slot, j, h] * qb_ref[h], axis=0, keepdims=True) for h in range(N_HEADS)]
        pages.append(jnp.concatenate(rows, axis=0))
    width = cp * page
    col0 = pl.multiple_of(c * width, width)
    s = jnp.concatenate(pages, axis=1) + bias_ref[:, pl.ds(col0, width)] + madd_ref[0, :, pl.ds(col0, width)]
    m_old = m_ref[...]
    m_new = jnp.maximum(m_old, _lane_rep(jnp.max(s, axis=1, keepdims=True)))
    alpha = jnp.exp(m_old - m_new)
    p = jnp.exp(s - m_new[:, 0:1])
    l_ref[...] = alpha * l_ref[...] + _lane_rep(jnp.sum(p, axis=1, keepdims=True))
    m_ref[...] = m_new
    for h in range(N_HEADS):
        acc = acc_ref[h] * jnp.broadcast_to(alpha[h:h + 1, :], (HEAD_DIM, page))
        for j in range(cp):
            acc = acc + vbuf[slot, j, h] * jnp.broadcast_to(p[h:h + 1, j * page:(j + 1) * page], (HEAD_DIM, page))
        acc_ref[h] = acc

    @pl.when(c == nc - 1)
    def _():
        s_new = (jnp.sum(q_ref[0] * kn_ref[0], axis=1, keepdims=True)
                 + bias_ref[:, past:past + 1] + madd_ref[0, :, past:past + 1])
        m_prev = m_ref[...]
        m_fin = jnp.maximum(m_prev, s_new)
        a_fin = jnp.exp(m_prev - m_fin)
        p_new = jnp.exp(s_new - m_fin)
        l_fin = a_fin * l_ref[...] + p_new
        lane = lax.broadcasted_iota(I32, (HEAD_DIM, LANES), 1)
        out = jnp.zeros((HEAD_DIM, LANES), F32)
        vnt = vnt_ref[0]
        for h in range(N_HEADS):
            tot = jnp.sum(acc_ref[h], axis=1, keepdims=True)
            col = (tot * a_fin[h:h + 1, 0:1] + p_new[h:h + 1, 0:1] * vnt[:, h:h + 1]) / l_fin[h:h + 1, 0:1]
            out = jnp.where(lane == h, col, out)
        o_ref[0] = out


def _sample_attention(page_table, q, k_new, v_new, madd, bias_s, cache_k_t, cache_v_t, layer, *, pages_per_step):
    nb, n_pages = page_table.shape
    page = cache_k_t.shape[4]
    past = n_pages * page
    cp = pages_per_step
    assert n_pages % cp == 0 and page == LANES
    head_blk = pl.BlockSpec((1, N_HEADS, HEAD_DIM), lambda i, j, pt: (i, 0, 0))
    tr_blk = pl.BlockSpec((1, HEAD_DIM, N_HEADS), lambda i, j, pt: (i, 0, 0))
    grid_spec = pltpu.PrefetchScalarGridSpec(
        num_scalar_prefetch=1,
        grid=(nb, n_pages // cp),
        in_specs=[head_blk, tr_blk, head_blk, tr_blk,
                  pl.BlockSpec((1, 1, past + LANES), lambda i, j, pt: (i, 0, 0)),
                  pl.BlockSpec(bias_s.shape, lambda i, j, pt: (0, 0)),
                  pl.BlockSpec(memory_space=pl.ANY), pl.BlockSpec(memory_space=pl.ANY)],
        out_specs=pl.BlockSpec((1, HEAD_DIM, LANES), lambda i, j, pt: (i, 0, 0)),
        scratch_shapes=[pltpu.VMEM((2, cp, N_HEADS, HEAD_DIM, page), F32),
                        pltpu.VMEM((2, cp, N_HEADS, HEAD_DIM, page), F32),
                        pltpu.VMEM((N_HEADS, HEAD_DIM, page), F32), pltpu.VMEM((N_HEADS, HEAD_DIM, page), F32),
                        pltpu.VMEM((N_HEADS, LANES), F32), pltpu.VMEM((N_HEADS, LANES), F32),
                        pltpu.SemaphoreType.DMA((2, 2))],
    )
    return pl.pallas_call(
        functools.partial(_sattn_kernel, layer=layer, cp=cp, page=page, past=past),
        grid_spec=grid_spec,
        out_shape=jax.ShapeDtypeStruct((nb, HEAD_DIM, LANES), F32),
        compiler_params=_params("arbitrary", "arbitrary"),
        name="sample_attention",
    )(page_table, q, jnp.swapaxes(q, 1, 2), k_new, jnp.swapaxes(v_new, 1, 2), madd, bias_s, cache_k_t, cache_v_t)


def _mix_kernel(x_ref, ya_ref, mc_ref, gb_ref, ga1_ref, sc2_ref, sh2_ref, gpost_ref, gpre_ref,
                wao_ref, wmo_ref, wr_ref, br_ref, x1_ref, h2_ref, te_ref, gates_ref):
    x = x_ref[0]
    t, d = x.shape
    attn = jnp.dot(ya_ref[0].astype(BF16), wao_ref[...], preferred_element_type=F32)
    merged = mc_ref[0] + gb_ref[0] * attn
    z = jnp.dot(merged.astype(BF16), wmo_ref[...], preferred_element_type=F32)
    x1 = x + ga1_ref[0] * _rms(z, gpost_ref[...])
    x1_ref[0] = x1
    h2 = _rms(x1, gpre_ref[...]) * (1.0 + sc2_ref[0]) + sh2_ref[0]
    for j in range(d // LANES):
        h2_ref[:, j, :] = h2[:, j * LANES:(j + 1) * LANES]
    lane = lax.broadcasted_iota(I32, (t, LANES), 1)
    logits = jnp.dot(h2.astype(BF16), wr_ref[...], preferred_element_type=F32) + br_ref[...]
    work = jnp.where(lane < N_EXPERTS, logits, -jnp.inf)
    top_l, top_e = [], []
    for _ in range(TOP_K_EXPERTS):
        mk = jnp.max(work, axis=1, keepdims=True)
        ek = jnp.min(jnp.where(work == mk, lane, LANES), axis=1, keepdims=True)
        top_l.append(mk)
        top_e.append(ek)
        work = jnp.where(lane == ek, -jnp.inf, work)
    ex = [jnp.exp(tl - top_l[0]) for tl in top_l]
    denom = ex[0] + ex[1] + ex[2] + ex[3]
    te = jnp.zeros((t, LANES), I32)
    gates = jnp.zeros((t, LANES), F32)
    for k in range(TOP_K_EXPERTS):
        te = jnp.where(lane == k, top_e[k], te)
        gates = jnp.where(lane == k, ex[k] / denom, gates)
    te_ref[...] = te
    gates_ref[...] = gates


def _mix_and_route(x, ya, mc, gb, ga1, sc2, sh2, gpost, gpre, wao, wmo, wr, br, *, rows):
    b, t, d = x.shape
    nblk = t // rows
    per_row = ga1.shape[1] != 1

    def row_spec(width, rowwise=True):
        if rowwise:
            return pl.BlockSpec((1, rows, width), lambda i, j: (i, j, 0))
        return pl.BlockSpec((1, 1, width), lambda i, j: (i, 0, 0))

    flat = lambda w: pl.BlockSpec((rows, w), lambda i, j: (i * nblk + j, 0))
    n = b * t
    return pl.pallas_call(
        _mix_kernel,
        grid=(b, nblk),
        in_specs=[row_spec(d), row_spec(D_ATTN), row_spec(d), row_spec(d),
                  row_spec(d, per_row), row_spec(d, per_row), row_spec(d, per_row),
                  _const_spec(gpost.shape), _const_spec(gpre.shape), _const_spec(wao.shape),
                  _const_spec(wmo.shape), _const_spec(wr.shape), _const_spec(br.shape)],
        out_specs=[row_spec(d),
                   pl.BlockSpec((rows, d // LANES, LANES), lambda i, j: (i * nblk + j, 0, 0)),
                   flat(LANES), flat(LANES)],
        out_shape=[jax.ShapeDtypeStruct((b, t, d), F32),
                   jax.ShapeDtypeStruct((n, d // LANES, LANES), F32),
                   jax.ShapeDtypeStruct((n, LANES), I32),
                   jax.ShapeDtypeStruct((n, LANES), F32)],
        compiler_params=_params("arbitrary", "arbitrary"),
        name="mix_and_route",
    )(x, ya, mc, gb, ga1, sc2, sh2, gpost, gpre, wao, wmo, wr, br)


def _rank_kernel(te_ref, rank_ref, cnt_ref, carry_ref):
    @pl.when(pl.program_id(0) == 0)
    def _():
        carry_ref[...] = jnp.zeros(carry_ref.shape, F32)

    te = te_ref[...]
    t = te.shape[0]
    lane = lax.broadcasted_iota(I32, (t, LANES), 1)
    hits = [lane == te[:, k:k + 1] for k in range(TOP_K_EXPERTS)]
    onehot = jnp.zeros((t, LANES), F32)
    for hit in hits:
        onehot = onehot + jnp.where(hit, 1.0, 0.0)
    lower = jnp.where(lax.broadcasted_iota(I32, (t, t), 0) > lax.broadcasted_iota(I32, (t, t), 1), 1.0, 0.0)
    before = jnp.dot(lower.astype(BF16), onehot.astype(BF16), preferred_element_type=F32) + carry_ref[...]
    rank = jnp.zeros((t, LANES), F32)
    for k, hit in enumerate(hits):
        rk = jnp.sum(jnp.where(hit, before, 0.0), axis=1, keepdims=True)
        rank = jnp.where(lane == k, rk, rank)
    rank_ref[...] = rank.astype(I32)
    carry_ref[...] = carry_ref[...] + jnp.sum(onehot, axis=0, keepdims=True)
    cnt_ref[...] = jnp.broadcast_to(carry_ref[...], cnt_ref.shape)


def _expert_ranks(te, *, rows):
    n = te.shape[0]
    return pl.pallas_call(
        _rank_kernel,
        grid=(n // rows,),
        in_specs=[pl.BlockSpec((rows, LANES), lambda i: (i, 0))],
        out_specs=[pl.BlockSpec((rows, LANES), lambda i: (i, 0)),
                   pl.BlockSpec((SUBLANES, LANES), lambda i: (0, 0))],
        out_shape=[jax.ShapeDtypeStruct((n, LANES), I32), jax.ShapeDtypeStruct((SUBLANES, LANES), F32)],
        scratch_shapes=[pltpu.VMEM((1, LANES), F32)],
        compiler_params=_params("arbitrary"),
        name="expert_ranks",
    )(te)


def _scatter_kernel(dest_ref, h2_ref, init_ref, xs_ref, sem, *, rows):
    del init_ref
    base = pl.program_id(0) * rows * TOP_K_EXPERTS

    def row_copy(n, k):
        return pltpu.make_async_copy(h2_ref.at[n], xs_ref.at[dest_ref[base + n * TOP_K_EXPERTS + k]], sem)

    def start(n, carry):
        for k in range(TOP_K_EXPERTS):
            row_copy(n, k).start()
        return carry

    def wait(n, carry):
        for k in range(TOP_K_EXPERTS):
            row_copy(n, k).wait()
        return carry

    lax.fori_loop(0, rows, start, 0)
    lax.fori_loop(0, rows, wait, 0)


def _scatter_rows(dest_flat, h2, cap, *, rows):
    n, s, _ = h2.shape
    init = jnp.zeros((cap, s, LANES), F32)
    grid_spec = pltpu.PrefetchScalarGridSpec(
        num_scalar_prefetch=1,
        grid=(n // rows,),
        in_specs=[pl.BlockSpec((rows, s, LANES), lambda i, d: (i, 0, 0)),
                  pl.BlockSpec(memory_space=pl.ANY)],
        out_specs=pl.BlockSpec(memory_space=pl.ANY),
        scratch_shapes=[pltpu.SemaphoreType.DMA(())],
    )
    return pl.pallas_call(
        functools.partial(_scatter_kernel, rows=rows),
        grid_spec=grid_spec,
        out_shape=jax.ShapeDtypeStruct((cap, s, LANES), F32),
        input_output_aliases={2: 0},
        compiler_params=_params("arbitrary"),
        name="scatter_rows",
    )(dest_flat, h2, init)


def _expert_kernel(be_ref, nu_ref, xs_ref, wgu_ref, bgu_ref, wdn_ref, bdn_ref, yb_ref):
    del be_ref

    @pl.when(pl.program_id(0) < nu_ref[0])
    def _():
        nseg = xs_ref.shape[1]
        x = jnp.concatenate([xs_ref[:, j, :] for j in range(nseg)], axis=1).astype(BF16)
        h = jnp.dot(x, wgu_ref[0], preferred_element_type=F32) + bgu_ref[0]
        gate = jnp.minimum(h[:, :D_FF], SWIGLU_LIMIT)
        up = jnp.clip(h[:, D_FF:], -SWIGLU_LIMIT, SWIGLU_LIMIT)
        act = (up + 1.0) * gate * jax.nn.sigmoid(SWIGLU_ALPHA * gate)
        y = jnp.dot(act.astype(BF16), wdn_ref[0], preferred_element_type=F32) + bdn_ref[0]
        for j in range(nseg):
            yb_ref[:, j, :] = y[:, j * LANES:(j + 1) * LANES]

    @pl.when(pl.program_id(0) >= nu_ref[0])
    def _():
        yb_ref[...] = jnp.zeros(yb_ref.shape, yb_ref.dtype)


def _expert_ffn(block_e, n_used, xs, w_gu, b_gu, w_dn, b_dn):
    cap, s, _ = xs.shape
    d = s * LANES
    nblk = cap // MOE_ROWS
    rows_map = lambda i, be, nu: (jnp.minimum(i, nu[0] - 1), 0, 0)
    by_expert = lambda i, be, nu: (be[i], 0, 0)
    grid_spec = pltpu.PrefetchScalarGridSpec(
        num_scalar_prefetch=2,
        grid=(nblk,),
        in_specs=[pl.BlockSpec((MOE_ROWS, s, LANES), rows_map),
                  pl.BlockSpec((1, d, 2 * D_FF), by_expert),
                  pl.BlockSpec((1, 1, 2 * D_FF), by_expert),
                  pl.BlockSpec((1, D_FF, d), by_expert),
                  pl.BlockSpec((1, 1, d), by_expert)],
        out_specs=pl.BlockSpec((MOE_ROWS, s, LANES), lambda i, be, nu: (i, 0, 0)),
    )
    return pl.pallas_call(
        _expert_kernel,
        grid_spec=grid_spec,
        out_shape=jax.ShapeDtypeStruct((cap, s, LANES), F32),
        compiler_params=_params("arbitrary"),
        name="expert_ffn",
    )(block_e, n_used, xs, w_gu, b_gu, w_dn, b_dn)


def _combine_kernel(dest_ref, gates_ref, x1_ref, ga2_ref, g_ref, yb_ref, o_ref, buf, sem, *, rows):
    nblk = pl.num_programs(1)
    base = (pl.program_id(0) * nblk + pl.program_id(1)) * rows * TOP_K_EXPERTS

    def row_copy(n, k):
        return pltpu.make_async_copy(yb_ref.at[dest_ref[base + n * TOP_K_EXPERTS + k]], buf.at[k, n], sem)

    def start(n, carry):
        for k in range(TOP_K_EXPERTS):
            row_copy(n, k).start()
        return carry

    def wait(n, carry):
        for k in range(TOP_K_EXPERTS):
            row_copy(n, k).wait()
        return carry

    lax.fori_loop(0, rows, start, 0)
    lax.fori_loop(0, rows, wait, 0)
    gates = gates_ref[...]
    nseg = buf.shape[2]
    f = None
    for k in range(TOP_K_EXPERTS):
        yk = jnp.concatenate([buf[k, :, j, :] for j in range(nseg)], axis=1)
        term = yk * gates[:, k:k + 1]
        f = term if f is None else f + term
    o_ref[0] = x1_ref[0] + ga2_ref[0] * _rms(f, g_ref[...])


def _combine(dest_flat, gates, x1, ga2, g, yb, *, rows):
    b, t, d = x1.shape
    nblk = t // rows
    per_row = ga2.shape[1] != 1
    row_blk = pl.BlockSpec((1, rows, d), lambda i, j, ds: (i, j, 0))
    mod_blk = row_blk if per_row else pl.BlockSpec((1, 1, d), lambda i, j, ds: (i, 0, 0))
    grid_spec = pltpu.PrefetchScalarGridSpec(
        num_scalar_prefetch=1,
        grid=(b, nblk),
        in_specs=[pl.BlockSpec((rows, LANES), lambda i, j, ds: (i * nblk + j, 0)),
                  row_blk, mod_blk,
                  pl.BlockSpec(g.shape, lambda i, j, ds: (0, 0)),
                  pl.BlockSpec(memory_space=pl.ANY)],
        out_specs=row_blk,
        scratch_shapes=[pltpu.VMEM((TOP_K_EXPERTS, rows, d // LANES, LANES), F32),
                        pltpu.SemaphoreType.DMA(())],
    )
    return pl.pallas_call(
        functools.partial(_combine_kernel, rows=rows),
        grid_spec=grid_spec,
        out_shape=jax.ShapeDtypeStruct((b, t, d), F32),
        compiler_params=_params("arbitrary", "arbitrary"),
        name="combine",
    )(dest_flat, gates, x1, ga2, g, yb)


def _moe(x1, h2, te, gates, ga2, g_post, w_gu, b_gu, w_dn, b_dn, *, rank_rows, io_rows):
    n = h2.shape[0]
    n_slots = n * TOP_K_EXPERTS
    rank, counts = _expert_ranks(te, rows=rank_rows)
    counts = counts[0, :N_EXPERTS].astype(I32)
    padded = (counts + MOE_ROWS - 1) // MOE_ROWS * MOE_ROWS
    pend = jnp.cumsum(padded)
    pstart = pend - padded
    nblk = (n_slots + N_EXPERTS * (MOE_ROWS - 1) + MOE_ROWS - 1) // MOE_ROWS
    n_used = (pend[-1] // MOE_ROWS).astype(I32)
    blk_start = jnp.arange(nblk, dtype=I32) * MOE_ROWS
    block_e = jnp.clip(jnp.searchsorted(pend, blk_start, side="right"), 0, N_EXPERTS - 1).astype(I32)
    block_e = jnp.where(jnp.arange(nblk) < n_used, block_e, block_e[jnp.maximum(n_used - 1, 0)])
    top_e = te[:, :TOP_K_EXPERTS]
    dest = (pstart[top_e] + rank[:, :TOP_K_EXPERTS]).reshape(-1).astype(I32)
    xs = _scatter_rows(dest, h2, nblk * MOE_ROWS, rows=io_rows)
    yb = _expert_ffn(block_e, n_used.reshape(1), xs, w_gu, b_gu, w_dn, b_dn)
    return _combine(dest, gates, x1, ga2, g_post, yb, rows=io_rows)


def _t5_bucket(dist):
    n = jnp.maximum(dist, 0)
    max_exact = N_BUCKETS // 2
    nf = jnp.maximum(n, 1).astype(F32)
    large = max_exact + (jnp.log(nf / max_exact) / math.log(MAX_DISTANCE / max_exact)
                         * (N_BUCKETS - max_exact)).astype(I32)
    large = jnp.minimum(large, N_BUCKETS - 1)
    return jnp.where(n < max_exact, n, large)


def _split_in_proj(w_in):
    d = w_in.shape[0]
    wa = w_in[:, :OFF_KI].astype(BF16)
    wk = jnp.pad(w_in[:, OFF_KI:OFF_GA], ((0, 0), (0, LANES - (OFF_GA - OFF_KI)))).astype(BF16)
    wga = w_in[:, OFF_GA:OFF_GA + d].astype(BF16)
    wgb = w_in[:, OFF_GA + d:OFF_GA + 2 * d].astype(BF16)
    return wa, wk, wga, wgb


def kernel(x_prompt, x_sample, cache_k, cache_v, cache_kidx, state_conv, page_table, c_prompt, c_sample, rel_bias, w_mod, b_mod, g_pre_mix, g_post_mix, w_in, conv_w, w_conv_out, w_attn_out, w_mix_out, g_pre_ffn, g_post_ffn, w_router, b_router, w_gu, b_gu, w_dn, b_dn):
    depth = w_mod.shape[0]
    bp, seq, d = x_prompt.shape
    nb, dec_seq, _ = x_sample.shape
    assert dec_seq == 1
    page = cache_k.shape[2]
    past = page_table.shape[1] * page
    tq, tk = 256, 512
    rows_p = 256
    s_topk = min(TOPK_MAX, (past + dec_seq) // 4)

    tab = rel_bias.astype(F32)[_t5_bucket(jnp.arange(2 * LANES + 1, dtype=I32))].T
    rel = tab - tab[:, MAX_DISTANCE:MAX_DISTANCE + 1]
    ii = jnp.arange(LANES, dtype=I32)[:, None]
    jj = jnp.arange(LANES, dtype=I32)[None, :]
    bias_tiles = jnp.stack([rel[:, jnp.clip(ii - jj, 0, 2 * LANES)],
                            rel[:, jnp.clip(LANES + ii - jj, 0, 2 * LANES)]], axis=1)
    bias_s = tab[:, jnp.clip(past - jnp.arange(past + LANES, dtype=I32), 0, MAX_DISTANCE)]
    cache_k_t = jnp.transpose(cache_k, (0, 1, 3, 4, 2))
    cache_v_t = jnp.transpose(cache_v, (0, 1, 3, 4, 2))
    cache_kidx_t = jnp.transpose(cache_kidx, (0, 1, 3, 2))

    xp = x_prompt
    xs_rows = x_sample.reshape(1, nb, d)
    c_all = jnp.concatenate([c_prompt, c_sample], axis=0)
    n_c = c_all.shape[0]
    c_all = jnp.pad(c_all, ((0, -n_c % SUBLANES), (0, 0)))
    outs = [[] for _ in range(8)]
    for l in range(depth):
        mod = _modulation(c_all, w_mod[l].astype(BF16), b_mod[l])
        mod_p = [m[:, None, :] for m in jnp.split(mod[:bp], 6, axis=-1)]
        mod_s = [m[None] for m in jnp.split(mod[bp:bp + nb], 6, axis=-1)]
        wa, wk, wga, wgb = _split_in_proj(w_in[l])
        wts = (wa, wk, wga, wgb, conv_w[l], w_conv_out[l].astype(BF16))
        g1 = g_pre_mix[l].reshape(1, d)
        wao = w_attn_out[l].astype(BF16)
        wmo = w_mix_out[l].astype(BF16)
        wr = jnp.pad(w_router[l], ((0, 0), (0, LANES - N_EXPERTS))).astype(BF16)
        br = jnp.pad(b_router[l], (0, LANES - N_EXPERTS)).reshape(1, LANES)
        gpost = g_post_mix[l].reshape(1, d)
        gpre2 = g_pre_ffn[l].reshape(1, d)
        gpost2 = g_post_ffn[l].reshape(1, d)
        wgu = w_gu[l].astype(BF16)
        bgu = b_gu[l].reshape(N_EXPERTS, 1, 2 * D_FF)
        wdn = w_dn[l].astype(BF16)
        bdn = b_dn[l].reshape(N_EXPERTS, 1, d)

        st = state_conv[l].reshape(1, nb, 2 * D_CONV)
        q, k, v, qi, kiwi, mc, gb, u = _in_projection(
            xs_rows, mod_s[1], mod_s[0], st, g1, wts, seq_conv=False, rows=nb)
        kidx = kiwi[0, :, :IDX_DIM]
        wi = kiwi[0, :, IDX_DIM:IDX_DIM + N_IDX_HEADS]
        scores = _sample_scores(page_table, qi.reshape(nb, N_IDX_HEADS, IDX_DIM), wi[:, :, None],
                                kidx[:, None, :], cache_kidx_t, l)
        madd = _sample_select(scores.reshape(nb, past + LANES), s_topk)
        k3 = k.reshape(nb, N_HEADS, HEAD_DIM)
        v3 = v.reshape(nb, N_HEADS, HEAD_DIM)
        ya = _sample_attention(page_table, q.astype(F32).reshape(nb, N_HEADS, HEAD_DIM), k3, v3,
                               madd.reshape(nb, 1, past + LANES), bias_s, cache_k_t, cache_v_t, l,
                               pages_per_step=16)
        ya = jnp.swapaxes(ya[:, :, :N_HEADS], 1, 2)
        x1, h2, te, gates = _mix_and_route(xs_rows, ya.reshape(1, nb, D_ATTN), mc, gb, mod_s[2], mod_s[4],
                                           mod_s[3], gpost, gpre2, wao, wmo, wr, br, rows=nb)
        xs_rows = _moe(x1, h2, te, gates, mod_s[5], gpost2, wgu, bgu, wdn, bdn, rank_rows=nb, io_rows=nb)
        outs[4].append(k.reshape(nb, dec_seq, N_HEADS, HEAD_DIM))
        outs[5].append(v.reshape(nb, dec_seq, N_HEADS, HEAD_DIM))
        outs[6].append(kidx[:, None, :])
        outs[7].append(jnp.stack([state_conv[l][:, 1, :], u[0]], axis=1))

        prefix = jnp.zeros((bp, 1, 2 * D_CONV), F32)
        q, k, v, qi, kiwi, mc, gb, u_tail = _in_projection(
            xp, mod_p[1], mod_p[0], prefix, g1, wts, seq_conv=True, rows=rows_p)
        kidx = kiwi[..., :IDX_DIM]
        by_head = lambda a: jnp.swapaxes(a.reshape(bp, seq, N_HEADS, HEAD_DIM), 1, 2)
        ya = _prompt_attention(by_head(q), by_head(qi), kiwi, jnp.swapaxes(kidx, 1, 2).astype(BF16),
                               jnp.swapaxes(k, 1, 2).astype(BF16), v.astype(BF16), bias_tiles,
                               tq=tq, tk=tk)
        x1, h2, te, gates = _mix_and_route(xp, ya, mc, gb, mod_p[2], mod_p[4], mod_p[3], gpost, gpre2,
                                           wao, wmo, wr, br, rows=rows_p)
        xp = _moe(x1, h2, te, gates, mod_p[5], gpost2, wgu, bgu, wdn, bdn, rank_rows=512, io_rows=rows_p)
        outs[0].append(k.reshape(bp, seq, N_HEADS, HEAD_DIM))
        outs[1].append(v.reshape(bp, seq, N_HEADS, HEAD_DIM))
        outs[2].append(kidx)
        outs[3].append(u_tail[:, SUBLANES - (CONV_W - 1):, :])
    return (xp, xs_rows.reshape(nb, dec_seq, d)) + tuple(jnp.stack(o) for o in outs)
```

```python
import functools
import math

import jax
import jax.numpy as jnp
from jax import lax
from jax.experimental import pallas as pl
from jax.experimental.pallas import tpu as pltpu

F32 = jnp.float32
BF16 = jnp.bfloat16
I32 = jnp.int32

D_CONV = 512
CONV_W = 3
N_HEADS = 8
HEAD_DIM = 64
D_ATTN = N_HEADS * HEAD_DIM
ATTN_SCALE = HEAD_DIM ** -0.5
N_IDX_HEADS = 8
IDX_DIM = 64
INDEX_SCALE = (N_IDX_HEADS * IDX_DIM) ** -0.5
TOPK_MAX = 256
N_BUCKETS = 32
MAX_DISTANCE = 128
N_EXPERTS = 32
TOP_K_EXPERTS = 4
D_FF = 1024
SWIGLU_LIMIT = 7.0
SWIGLU_ALPHA = 1.702
EPS = 1e-6

LANES = 128
SUBLANES = 8
VMEM_LIMIT_BYTES = 56 * 1024 * 1024

OFF_Q = 3 * D_CONV
OFF_K = OFF_Q + D_ATTN
OFF_V = OFF_K + D_ATTN
OFF_QI = OFF_V + D_ATTN
OFF_KI = OFF_QI + N_IDX_HEADS * IDX_DIM
OFF_WI = OFF_KI + IDX_DIM
OFF_GA = OFF_WI + N_IDX_HEADS

LOG2E = 1.4426950408889634
INT_MIN = -2 ** 31
NEG_BIG = -1e30
MOE_ROWS = 512


def _params(*sem):
    return pltpu.CompilerParams(dimension_semantics=sem, vmem_limit_bytes=VMEM_LIMIT_BYTES)


def _const_spec(shape):
    zeros = (0,) * len(shape)
    return pl.BlockSpec(shape, lambda *_: zeros, pipeline_mode=pl.Buffered(1))


def _rms(x, g):
    ms = jnp.mean(x * x, axis=-1, keepdims=True)
    return (x * lax.rsqrt(ms + EPS)) * g


def _sort_key(x):
    bits = pltpu.bitcast(x, I32)
    return jnp.where(bits < 0, bits ^ jnp.int32(0x7FFFFFFF), bits)


def _mod_kernel(c_ref, w_ref, b_ref, o_ref):
    c = c_ref[...]
    s = c * jax.nn.sigmoid(c)
    o_ref[...] = jnp.dot(s.astype(BF16), w_ref[...], preferred_element_type=F32) + b_ref[...]


def _modulation(c, w_mod, b_mod):
    n, d = c.shape
    n_out = w_mod.shape[1]
    bn = 1024
    return pl.pallas_call(
        _mod_kernel,
        grid=(n_out // bn,),
        in_specs=[pl.BlockSpec((n, d), lambda j: (0, 0)),
                  pl.BlockSpec((d, bn), lambda j: (0, j)),
                  pl.BlockSpec((1, bn), lambda j: (0, j))],
        out_specs=pl.BlockSpec((n, bn), lambda j: (0, j)),
        out_shape=jax.ShapeDtypeStruct((n, n_out), F32),
        compiler_params=_params("arbitrary"),
        name="modulation",
    )(c, w_mod, b_mod.reshape(1, n_out))


def _inproj_kernel(x_ref, sc_ref, sh_ref, st_ref, g_ref, wa_ref, wk_ref, wga_ref, wgb_ref, cw_ref, wco_ref,
                   q_ref, k_ref, v_ref, qi_ref, kiwi_ref, mc_ref, gb_ref, u_ref, carry_ref, *, seq_conv):
    x = x_ref[0]
    t = x.shape[0]
    h = _rms(x, g_ref[...]) * (1.0 + sc_ref[0]) + sh_ref[0]
    hb = h.astype(BF16)

    def proj(lo, hi):
        return jnp.dot(hb, wa_ref[:, lo:hi], preferred_element_type=F32)

    b_gate = proj(0, D_CONV)
    u = proj(D_CONV, 2 * D_CONV) * proj(2 * D_CONV, 3 * D_CONV)
    if seq_conv:
        @pl.when(pl.program_id(1) == 0)
        def _():
            carry_ref[0:1, :] = st_ref[0][:, 0:D_CONV]
            carry_ref[1:2, :] = st_ref[0][:, D_CONV:2 * D_CONV]

        cm2 = carry_ref[0:1, :]
        cm1 = carry_ref[1:2, :]
        rows = lax.broadcasted_iota(I32, u.shape, 0)
        prev1 = jnp.where(rows == 0, cm1, pltpu.roll(u, 1, 0))
        prev2 = jnp.where(rows == 0, cm2, jnp.where(rows == 1, cm1, pltpu.roll(u, 2, 0)))
        carry_ref[...] = u[t - 2:t, :]
        u_ref[0] = u[t - SUBLANES:t, :]
    else:
        prev2 = st_ref[0][:, 0:D_CONV]
        prev1 = st_ref[0][:, D_CONV:2 * D_CONV]
        u_ref[0] = u
    cw = cw_ref[...]
    y_conv = b_gate * (cw[0:1, :] * prev2 + cw[1:2, :] * prev1 + cw[2:3, :] * u)
    g_a = jax.nn.sigmoid(jnp.dot(hb, wga_ref[...], preferred_element_type=F32))
    mc_ref[0] = g_a * jnp.dot(y_conv.astype(BF16), wco_ref[...], preferred_element_type=F32)
    gb_ref[0] = jax.nn.sigmoid(jnp.dot(hb, wgb_ref[...], preferred_element_type=F32))
    q_ref[0] = (proj(OFF_Q, OFF_K) * ATTN_SCALE).astype(BF16)
    k_ref[0] = proj(OFF_K, OFF_V)
    v_ref[0] = proj(OFF_V, OFF_QI)
    qi_ref[0] = proj(OFF_QI, OFF_KI).astype(BF16)
    kiwi_ref[0] = jnp.dot(hb, wk_ref[...], preferred_element_type=F32)


def _in_projection(x, sc, sh, st, g, wts, *, seq_conv, rows):
    b, t, d = x.shape
    nblk = t // rows
    tm = 1 if sc.shape[1] == 1 else rows
    ts = 1 if seq_conv else rows
    tu = SUBLANES if seq_conv else rows
    wa, wk, wga, wgb, cw, wco = wts

    def row_spec(width, per_row):
        if per_row:
            return pl.BlockSpec((1, rows, width), lambda i, j: (i, j, 0))
        return pl.BlockSpec((1, 1, width), lambda i, j: (i, 0, 0))

    out_widths = (D_ATTN, D_ATTN, D_ATTN, N_IDX_HEADS * IDX_DIM, LANES, d, d)
    out_dtypes = (BF16, F32, F32, BF16, F32, F32, F32)
    out_shape = [jax.ShapeDtypeStruct((b, t, w), dt) for w, dt in zip(out_widths, out_dtypes)]
    out_specs = [row_spec(w, True) for w in out_widths]
    if seq_conv:
        out_shape.append(jax.ShapeDtypeStruct((b, tu, D_CONV), F32))
        out_specs.append(pl.BlockSpec((1, tu, D_CONV), lambda i, j: (i, 0, 0)))
    else:
        out_shape.append(jax.ShapeDtypeStruct((b, t, D_CONV), F32))
        out_specs.append(row_spec(D_CONV, True))
    return pl.pallas_call(
        functools.partial(_inproj_kernel, seq_conv=seq_conv),
        grid=(b, nblk),
        in_specs=[row_spec(d, True), row_spec(d, tm != 1), row_spec(d, tm != 1),
                  row_spec(2 * D_CONV, ts != 1),
                  _const_spec(g.shape), _const_spec(wa.shape), _const_spec(wk.shape),
                  _const_spec(wga.shape), _const_spec(wgb.shape), _const_spec(cw.shape),
                  _const_spec(wco.shape)],
        out_specs=out_specs,
        out_shape=out_shape,
        scratch_shapes=[pltpu.VMEM((2, D_CONV), F32)],
        compiler_params=_params("arbitrary", "arbitrary"),
        name="in_projection",
    )(x, sc, sh, st, g, wa, wk, wga, wgb, cw, wco)


ROW_TILE = 128
ATT_ROWS = 256


def _lane_rep(col, width=LANES):
    return jnp.broadcast_to(col, (col.shape[0], width))


def _tile_lanes(x, n):
    return x if n == 1 else jnp.concatenate([x] * n, axis=1)


def _count_rows(keys_ref, nblk, tq, tk, pred_of_tile):
    out = []
    for r in range(tq // ROW_TILE):
        rows = slice(r * ROW_TILE, (r + 1) * ROW_TILE)
        pred = pred_of_tile(r)

        def body(kb, acc, rows=rows, pred=pred):
            off = pl.multiple_of(kb * tk, tk)
            for c in range(tk // LANES):
                blk = keys_ref[rows, pl.ds(off + c * LANES, LANES)]
                acc = acc + jnp.where(pred(blk), 1.0, 0.0)
            return acc

        out.append(lax.fori_loop(0, nblk, body, jnp.zeros((ROW_TILE, LANES), F32)))
    assert keys_ref.shape[1] // LANES < 256
    partial = jnp.concatenate(out, axis=0).astype(BF16)
    return jnp.dot(partial, jnp.ones((LANES, LANES), BF16), preferred_element_type=F32).astype(I32)


def _kth_largest_key(keys_ref, nblk, tq, tk, k):
    def bit_body(i, kk):
        cand = kk + (jnp.int32(1) << (31 - i))

        def pred_of_tile(r):
            cb = cand[r * ROW_TILE:(r + 1) * ROW_TILE]
            return lambda blk: blk >= cb

        cnt = _count_rows(keys_ref, nblk, tq, tk, pred_of_tile)
        return jnp.where(cnt >= k, cand, kk)

    return lax.fori_loop(0, 32, bit_body, jnp.full((tq, LANES), INT_MIN, I32))


def _pattn_kernel(q_ref, qi_ref, kiwi_ref, kidxt_ref, kt_ref, v_ref, bias_ref, upper_ref, o_ref,
                  keys_ref, wb_ref, madd_ref, m_ref, acc_ref, eqc_ref, *, tq, tk, topk):
    qb = pl.program_id(1)
    kd = (qb * tq) // tk
    nkb = kd + 1
    reps = tk // LANES
    wi = kiwi_ref[0][:, IDX_DIM:IDX_DIM + N_IDX_HEADS]
    for h in range(N_IDX_HEADS):
        wb_ref[h] = _lane_rep(wi[:, h:h + 1])

    col_w = 2 * LANES
    row_i = lax.broadcasted_iota(I32, (ROW_TILE, col_w), 0)
    col_i = lax.broadcasted_iota(I32, (ROW_TILE, col_w), 1)

    def score_block(kb, carry):
        off = pl.multiple_of(kb * tk, tk)
        for r in range(tq // ROW_TILE):
            rows = slice(r * ROW_TILE, (r + 1) * ROW_TILE)
            for c in range(tk // col_w):
                kx = kidxt_ref[0, :, pl.ds(off + c * col_w, col_w)]
                acc = jnp.zeros((ROW_TILE, col_w), F32)
                for h in range(N_IDX_HEADS):
                    s = jnp.dot(qi_ref[0, h, rows, :], kx, preferred_element_type=F32)
                    acc = acc + jnp.maximum(s, 0.0) * _tile_lanes(wb_ref[h, rows, :], col_w // LANES)
                key = _sort_key(acc * INDEX_SCALE)
                valid = (off + c * col_w + col_i) <= (qb * tq + r * ROW_TILE + row_i)
                keys_ref[rows, pl.ds(off + c * col_w, col_w)] = jnp.where(valid, key, INT_MIN)
        return carry

    lax.fori_loop(0, nkb, score_block, 0)

    kth = _kth_largest_key(keys_ref, nkb, tq, tk, topk)

    def gt_of_tile(r):
        kb_ = kth[r * ROW_TILE:(r + 1) * ROW_TILE]
        return lambda blk: blk > kb_

    cnt_gt = _count_rows(keys_ref, nkb, tq, tk, gt_of_tile)
    need = jnp.where(kth == INT_MIN, 0, topk - cnt_gt).astype(F32)
    kth_w = _tile_lanes(kth, reps)
    need_w = _tile_lanes(need, reps)

    m_ref[...] = jnp.full(m_ref.shape, NEG_BIG, F32)
    acc_ref[...] = jnp.zeros(acc_ref.shape, F32)
    eqc_ref[...] = jnp.zeros(eqc_ref.shape, F32)

    def near_bias(h, kb, r):
        nrow = min(ATT_ROWS, LANES)
        strips = []
        for r0 in range(r * ATT_ROWS, (r + 1) * ATT_ROWS, nrow):
            i, sub = divmod(r0, LANES)
            t0 = bias_ref[h, 0, sub:sub + nrow, :]
            t1 = bias_ref[h, 1, sub:sub + nrow, :]
            tiles = []
            for j in range(reps):
                d = qb * (tq // LANES) + i - kb * reps - j
                tiles.append(jnp.where(d == 0, t0, jnp.where(d == 1, t1, 0.0)))
            strips.append(jnp.concatenate(tiles, axis=1))
        return strips[0] if len(strips) == 1 else jnp.concatenate(strips, axis=0)

    lane_k = lax.broadcasted_iota(I32, (tk, LANES), 1)
    own = [jnp.where((lane_k < HEAD_DIM) == (par == 0), 1.0, 0.0).astype(BF16) for par in (0, 1)]
    other = [own[1], own[0]]

    def attend_rows(off, rows, bias_of_head):
        madd = madd_ref[rows, :]
        for h in range(N_HEADS):
            s = jnp.dot(q_ref[0, h, rows, :], kt_ref[0, h * HEAD_DIM:(h + 1) * HEAD_DIM, pl.ds(off, tk)],
                        preferred_element_type=F32) + madd
            if bias_of_head is not None:
                s = s + bias_of_head(h)
            m_old = m_ref[h, rows, :]
            m_new = jnp.maximum(m_old, _lane_rep(jnp.max(s, axis=1, keepdims=True)))
            alpha = jnp.exp2(m_old - m_new)
            p = jnp.exp2(s - _tile_lanes(m_new, reps))
            pair = slice((h // 2) * LANES, (h // 2 + 1) * LANES)
            v_aug = v_ref[0, pl.ds(off, tk), pair] * own[h % 2] + other[h % 2]
            pv = jnp.dot(p.astype(BF16), v_aug, preferred_element_type=F32)
            acc_ref[h, rows, :] = alpha * acc_ref[h, rows, :] + pv
            m_ref[h, rows, :] = m_new

    def attend_block(kb, near):
        off = pl.multiple_of(kb * tk, tk)
        sk = keys_ref[:, pl.ds(off, tk)]
        eq = sk == kth_w
        eqf = jnp.where(eq, 1.0, 0.0)
        rank = jnp.dot(eqf.astype(BF16), upper_ref[...], preferred_element_type=F32)
        rank = rank + _tile_lanes(eqc_ref[...], reps)
        eqc_ref[...] = eqc_ref[...] + _lane_rep(jnp.sum(eqf, axis=1, keepdims=True))
        madd_ref[...] = jnp.where(sk > kth_w, 0.0, jnp.where(eq, jnp.where(rank < need_w, 0.0, NEG_BIG), NEG_BIG))
        for r in range(tq // ATT_ROWS):
            bias_of_head = (lambda h, r=r: near_bias(h, kb, r)) if near else None
            attend_rows(off, slice(r * ATT_ROWS, (r + 1) * ATT_ROWS), bias_of_head)

    def far_block(kb, carry):
        attend_block(kb, False)
        return carry

    lax.fori_loop(0, jnp.maximum(kd - 1, 0), far_block, 0)

    @pl.when(kd >= 1)
    def _():
        attend_block(kd - 1, True)

    attend_block(kd, True)
    lane_q = lax.broadcasted_iota(I32, (tq, LANES), 1)
    for j in range(N_HEADS // 2):
        even = acc_ref[2 * j]
        odd = acc_ref[2 * j + 1]
        out = jnp.where(lane_q < HEAD_DIM, even / pltpu.roll(even, HEAD_DIM, 1), odd / pltpu.roll(odd, HEAD_DIM, 1))
        o_ref[0, :, j * LANES:(j + 1) * LANES] = out.astype(o_ref.dtype)


def _prompt_attention(q, qi, kiwi, kidxt, kt, v, bias_tiles, *, tq, tk):
    b, _, s, _ = q.shape
    topk = min(TOPK_MAX, s // 4)
    assert tk % (2 * LANES) == 0 and tk % tq == 0 and tq % ROW_TILE == 0 and s % tk == 0
    assert LANES >= MAX_DISTANCE and tk >= 2 * LANES
    upper = jnp.triu(jnp.ones((tk, tk), BF16), 1)
    heads = lambda: pl.BlockSpec((1, N_HEADS, tq, HEAD_DIM), lambda i, j: (i, 0, j, 0))
    per_b = lambda r, c: pl.BlockSpec((1, r, c), lambda i, j: (i, 0, 0), pipeline_mode=pl.Buffered(1))
    stat = lambda w: pltpu.VMEM((N_HEADS, tq, w), F32)
    return pl.pallas_call(
        functools.partial(_pattn_kernel, tq=tq, tk=tk, topk=topk),
        grid=(b, s // tq),
        in_specs=[heads(), heads(), pl.BlockSpec((1, tq, LANES), lambda i, j: (i, j, 0)),
                  per_b(IDX_DIM, s), per_b(D_ATTN, s), per_b(s, D_ATTN),
                  _const_spec(bias_tiles.shape), _const_spec(upper.shape)],
        out_specs=pl.BlockSpec((1, tq, D_ATTN), lambda i, j: (i, j, 0)),
        out_shape=jax.ShapeDtypeStruct((b, s, D_ATTN), BF16),
        scratch_shapes=[pltpu.VMEM((tq, s), I32), stat(LANES), pltpu.VMEM((tq, tk), F32),
                        stat(LANES), stat(LANES), pltpu.VMEM((tq, LANES), F32)],
        compiler_params=_params("arbitrary", "arbitrary"),
        name="prompt_attention",
    )(q, qi, kiwi, kidxt, kt, v, bias_tiles, upper)


def _sscore_kernel(pt_ref, qi_ref, wi_ref, kn_ref, cache_ref, o_ref, buf, sem, *, layer, n_pages, page):
    b = pl.program_id(0)
    nb = pl.num_programs(0)
    past = n_pages * page

    def page_copy(bb, slot, p):
        return pltpu.make_async_copy(cache_ref.at[layer, pt_ref[bb, p]], buf.at[slot, p], sem.at[slot])

    def fetch(bb, slot):
        for p in range(n_pages):
            page_copy(bb, slot, p).start()

    @pl.when(b == 0)
    def _():
        fetch(0, 0)

    @pl.when(b + 1 < nb)
    def _():
        fetch(b + 1, (b + 1) % 2)

    slot = b % 2
    for p in range(n_pages):
        page_copy(b, slot, p).wait()
    qi = qi_ref[0]
    wi = wi_ref[0]
    group = 8
    for g in range(n_pages // group):
        kx = jnp.concatenate([buf[slot, g * group + j] for j in range(group)], axis=1).astype(BF16)
        s = jnp.dot(qi, kx, preferred_element_type=F32)
        sc = jnp.sum(jnp.maximum(s, 0.0) * wi, axis=0, keepdims=True) * INDEX_SCALE
        o_ref[0, :, g * group * page:(g + 1) * group * page] = sc
    kn = kn_ref[0].astype(BF16).astype(F32)
    sn = jnp.sum(qi.astype(F32) * kn, axis=1, keepdims=True)
    s_new = jnp.sum(jnp.maximum(sn, 0.0) * wi, axis=0, keepdims=True) * INDEX_SCALE
    lane = lax.broadcasted_iota(I32, (1, LANES), 1)
    o_ref[0, :, past:past + LANES] = jnp.where(lane == 0, s_new, -jnp.inf)


def _sample_scores(page_table, qi, wi, kidx_new, cache_kidx_t, layer):
    nb, n_pages = page_table.shape
    page = cache_kidx_t.shape[3]
    past = n_pages * page
    assert n_pages % 8 == 0 and page == LANES
    grid_spec = pltpu.PrefetchScalarGridSpec(
        num_scalar_prefetch=1,
        grid=(nb,),
        in_specs=[pl.BlockSpec((1, N_IDX_HEADS, IDX_DIM), lambda i, pt: (i, 0, 0)),
                  pl.BlockSpec((1, N_IDX_HEADS, 1), lambda i, pt: (i, 0, 0)),
                  pl.BlockSpec((1, 1, IDX_DIM), lambda i, pt: (i, 0, 0)),
                  pl.BlockSpec(memory_space=pl.ANY)],
        out_specs=pl.BlockSpec((1, 1, past + LANES), lambda i, pt: (i, 0, 0)),
        scratch_shapes=[pltpu.VMEM((2, n_pages, IDX_DIM, page), F32), pltpu.SemaphoreType.DMA((2,))],
    )
    return pl.pallas_call(
        functools.partial(_sscore_kernel, layer=layer, n_pages=n_pages, page=page),
        grid_spec=grid_spec,
        out_shape=jax.ShapeDtypeStruct((nb, 1, past + LANES), F32),
        compiler_params=_params("arbitrary"),
        name="sample_scores",
    )(page_table, qi, wi, kidx_new, cache_kidx_t)


def _ssel_kernel(sc_ref, madd_ref, keys_ref, *, topk):
    nb, width = sc_ref.shape
    nblk = width // LANES
    keys_ref[...] = _sort_key(sc_ref[...])

    def count(pred):
        def body(c, acc):
            off = pl.multiple_of(c * LANES, LANES)
            return acc + jnp.where(pred(keys_ref[:, pl.ds(off, LANES)]), 1, 0)
        acc = lax.fori_loop(0, nblk, body, jnp.zeros((nb, LANES), I32))
        return jnp.sum(acc, axis=1, keepdims=True)

    def bit_body(i, kk):
        cand = kk + (jnp.int32(1) << (31 - i))
        return jnp.where(count(lambda blk: blk >= cand) >= topk, cand, kk)

    kth = lax.fori_loop(0, 32, bit_body, jnp.full((nb, 1), INT_MIN, I32))
    need = (topk - count(lambda blk: blk > kth)).astype(F32)
    r_i = lax.broadcasted_iota(I32, (LANES, LANES), 0)
    c_i = lax.broadcasted_iota(I32, (LANES, LANES), 1)
    upper = jnp.where(r_i < c_i, 1.0, 0.0).astype(BF16)

    def mask_block(c, eqc):
        off = pl.multiple_of(c * LANES, LANES)
        sk = keys_ref[:, pl.ds(off, LANES)]
        eq = sk == kth
        eqf = jnp.where(eq, 1.0, 0.0)
        rank = jnp.dot(eqf.astype(BF16), upper, preferred_element_type=F32) + eqc
        madd_ref[:, pl.ds(off, LANES)] = jnp.where(
            sk > kth, 0.0, jnp.where(eq, jnp.where(rank < need, 0.0, NEG_BIG), NEG_BIG))
        return eqc + jnp.sum(eqf, axis=1, keepdims=True)

    lax.fori_loop(0, nblk, mask_block, jnp.zeros((nb, 1), F32))


def _sample_select(scores, topk):
    nb, width = scores.shape
    return pl.pallas_call(
        functools.partial(_ssel_kernel, topk=topk),
        out_shape=jax.ShapeDtypeStruct((nb, width), F32),
        scratch_shapes=[pltpu.VMEM((nb, width), I32)],
        compiler_params=pltpu.CompilerParams(vmem_limit_bytes=VMEM_LIMIT_BYTES),
        name="sample_select",
    )(scores)


def _sattn_kernel(pt_ref, q_ref, qt_ref, kn_ref, vnt_ref, madd_ref, bias_ref, ck_ref, cv_ref, o_ref,
                  kbuf, vbuf, qb_ref, acc_ref, m_ref, l_ref, sem, *, layer, cp, page, past):
    b = pl.program_id(0)
    c = pl.program_id(1)
    nc = pl.num_programs(1)
    step = b * nc + c
    n_steps = pl.num_programs(0) * nc

    def page_copies(bb, cc, slot, j):
        phys = pt_ref[bb, cc * cp + j]
        return (pltpu.make_async_copy(ck_ref.at[layer, phys], kbuf.at[slot, j], sem.at[0, slot]),
                pltpu.make_async_copy(cv_ref.at[layer, phys], vbuf.at[slot, j], sem.at[1, slot]))

    def fetch(bb, cc, slot):
        for j in range(cp):
            ck, cv = page_copies(bb, cc, slot, j)
            ck.start()
            cv.start()

    @pl.when(step == 0)
    def _():
        fetch(0, 0, 0)

    nxt = step + 1

    @pl.when(nxt < n_steps)
    def _():
        fetch(nxt // nc, nxt % nc, nxt % 2)

    slot = step % 2
    for j in range(cp):
        ck, cv = page_copies(b, c, slot, j)
        ck.wait()
        cv.wait()

    @pl.when(c == 0)
    def _():
        m_ref[...] = jnp.full(m_ref.shape, NEG_BIG, F32)
        l_ref[...] = jnp.zeros(l_ref.shape, F32)
        acc_ref[...] = jnp.zeros(acc_ref.shape, F32)
        qt = qt_ref[0]
        for h in range(N_HEADS):
            qb_ref[h] = _lane_rep(qt[:, h:h + 1], page)

    pages = []
    for j in range(cp):
        rows = [jnp.sum(kbuf[slot, j, h] * qb_ref[h], axis=0, keepdims=True) for h in range(N_HEADS)]
        pages.append(jnp.concatenate(rows, axis=0))
    width = cp * page
    col0 = pl.multiple_of(c * width, width)
    s = jnp.concatenate(pages, axis=1) + bias_ref[:, pl.ds(col0, width)] + madd_ref[0, :, pl.ds(col0, width)]
    m_old = m_ref[...]
    m_new = jnp.maximum(m_old, _lane_rep(jnp.max(s, axis=1, keepdims=True)))
    alpha = jnp.exp(m_old - m_new)
    p = jnp.exp(s - m_new[:, 0:1])
    l_ref[...] = alpha * l_ref[...] + _lane_rep(jnp.sum(p, axis=1, keepdims=True))
    m_ref[...] = m_new
    for h in range(N_HEADS):
        acc = acc_ref[h] * jnp.broadcast_to(alpha[h:h + 1, :], (HEAD_DIM, page))
        for j in range(cp):
            acc = acc + vbuf[slot, j, h] * jnp.broadcast_to(p[h:h + 1, j * page:(j + 1) * page], (HEAD_DIM, page))
        acc_ref[h] = acc

    @pl.when(c == nc - 1)
    def _():
        s_new = (jnp.sum(q_ref[0] * kn_ref[0], axis=1, keepdims=True)
                 + bias_ref[:, past:past + 1] + madd_ref[0, :, past:past + 1])
        m_prev = m_ref[...]
        m_fin = jnp.maximum(m_prev, s_new)
        a_fin = jnp.exp(m_prev - m_fin)
        p_new = jnp.exp(s_new - m_fin)
        l_fin = a_fin * l_ref[...] + p_new
        lane = lax.broadcasted_iota(I32, (HEAD_DIM, LANES), 1)
        out = jnp.zeros((HEAD_DIM, LANES), F32)
        vnt = vnt_ref[0]
        for h in range(N_HEADS):
            tot = jnp.sum(acc_ref[h], axis=1, keepdims=True)
            col = (tot * a_fin[h:h + 1, 0:1] + p_new[h:h + 1, 0:1] * vnt[:, h:h + 1]) / l_fin[h:h + 1, 0:1]
            out = jnp.where(lane == h, col, out)
        o_ref[0] = out


def _sample_attention(page_table, q, k_new, v_new, madd, bias_s, cache_k_t, cache_v_t, layer, *, pages_per_step):
    nb, n_pages = page_table.shape
    page = cache_k_t.shape[4]
    past = n_pages * page
    cp = pages_per_step
    assert n_pages % cp == 0 and page == LANES
    head_blk = pl.BlockSpec((1, N_HEADS, HEAD_DIM), lambda i, j, pt: (i, 0, 0))
    tr_blk = pl.BlockSpec((1, HEAD_DIM, N_HEADS), lambda i, j, pt: (i, 0, 0))
    grid_spec = pltpu.PrefetchScalarGridSpec(
        num_scalar_prefetch=1,
        grid=(nb, n_pages // cp),
        in_specs=[head_blk, tr_blk, head_blk, tr_blk,
                  pl.BlockSpec((1, 1, past + LANES), lambda i, j, pt: (i, 0, 0)),
                  pl.BlockSpec(bias_s.shape, lambda i, j, pt: (0, 0)),
                  pl.BlockSpec(memory_space=pl.ANY), pl.BlockSpec(memory_space=pl.ANY)],
        out_specs=pl.BlockSpec((1, HEAD_DIM, LANES), lambda i, j, pt: (i, 0, 0)),
        scratch_shapes=[pltpu.VMEM((2, cp, N_HEADS, HEAD_DIM, page), F32),
                        pltpu.VMEM((2, cp, N_HEADS, HEAD_DIM, page), F32),
                        pltpu.VMEM((N_HEADS, HEAD_DIM, page), F32), pltpu.VMEM((N_HEADS, HEAD_DIM, page), F32),
                        pltpu.VMEM((N_HEADS, LANES), F32), pltpu.VMEM((N_HEADS, LANES), F32),
                        pltpu.SemaphoreType.DMA((2, 2))],
    )
    return pl.pallas_call(
        functools.partial(_sattn_kernel, layer=layer, cp=cp, page=page, past=past),
        grid_spec=grid_spec,
        out_shape=jax.ShapeDtypeStruct((nb, HEAD_DIM, LANES), F32),
        compiler_params=_params("arbitrary", "arbitrary"),
        name="sample_attention",
    )(page_table, q, jnp.swapaxes(q, 1, 2), k_new, jnp.swapaxes(v_new, 1, 2), madd, bias_s, cache_k_t, cache_v_t)


def _mix_kernel(x_ref, ya_ref, mc_ref, gb_ref, ga1_ref, sc2_ref, sh2_ref, gpost_ref, gpre_ref,
                wao_ref, wmo_ref, wr_ref, br_ref, x1_ref, h2_ref, te_ref, gates_ref):
    x = x_ref[0]
    t, d = x.shape
    attn = jnp.dot(ya_ref[0].astype(BF16), wao_ref[...], preferred_element_type=F32)
    merged = mc_ref[0] + gb_ref[0] * attn
    z = jnp.dot(merged.astype(BF16), wmo_ref[...], preferred_element_type=F32)
    x1 = x + ga1_ref[0] * _rms(z, gpost_ref[...])
    x1_ref[0] = x1
    h2 = _rms(x1, gpre_ref[...]) * (1.0 + sc2_ref[0]) + sh2_ref[0]
    for j in range(d // LANES):
        h2_ref[:, j, :] = h2[:, j * LANES:(j + 1) * LANES]
    lane = lax.broadcasted_iota(I32, (t, LANES), 1)
    logits = jnp.dot(h2.astype(BF16), wr_ref[...], preferred_element_type=F32) + br_ref[...]
    work = jnp.where(lane < N_EXPERTS, logits, -jnp.inf)
    top_l, top_e = [], []
    for _ in range(TOP_K_EXPERTS):
        mk = jnp.max(work, axis=1, keepdims=True)
        ek = jnp.min(jnp.where(work == mk, lane, LANES), axis=1, keepdims=True)
        top_l.append(mk)
        top_e.append(ek)
        work = jnp.where(lane == ek, -jnp.inf, work)
    ex = [jnp.exp(tl - top_l[0]) for tl in top_l]
    denom = ex[0] + ex[1] + ex[2] + ex[3]
    te = jnp.zeros((t, LANES), I32)
    gates = jnp.zeros((t, LANES), F32)
    for k in range(TOP_K_EXPERTS):
        te = jnp.where(lane == k, top_e[k], te)
        gates = jnp.where(lane == k, ex[k] / denom, gates)
    te_ref[...] = te
    gates_ref[...] = gates


def _mix_and_route(x, ya, mc, gb, ga1, sc2, sh2, gpost, gpre, wao, wmo, wr, br, *, rows):
    b, t, d = x.shape
    nblk = t // rows
    per_row = ga1.shape[1] != 1

    def row_spec(width, rowwise=True):
        if rowwise:
            return pl.BlockSpec((1, rows, width), lambda i, j: (i, j, 0))
        return pl.BlockSpec((1, 1, width), lambda i, j: (i, 0, 0))

    flat = lambda w: pl.BlockSpec((rows, w), lambda i, j: (i * nblk + j, 0))
    n = b * t
    return pl.pallas_call(
        _mix_kernel,
        grid=(b, nblk),
        in_specs=[row_spec(d), row_spec(D_ATTN), row_spec(d), row_spec(d),
                  row_spec(d, per_row), row_spec(d, per_row), row_spec(d, per_row),
                  _const_spec(gpost.shape), _const_spec(gpre.shape), _const_spec(wao.shape),
                  _const_spec(wmo.shape), _const_spec(wr.shape), _const_spec(br.shape)],
        out_specs=[row_spec(d),
                   pl.BlockSpec((rows, d // LANES, LANES), lambda i, j: (i * nblk + j, 0, 0)),
                   flat(LANES), flat(LANES)],
        out_shape=[jax.ShapeDtypeStruct((b, t, d), F32),
                   jax.ShapeDtypeStruct((n, d // LANES, LANES), F32),
                   jax.ShapeDtypeStruct((n, LANES), I32),
                   jax.ShapeDtypeStruct((n, LANES), F32)],
        compiler_params=_params("arbitrary", "arbitrary"),
        name="mix_and_route",
    )(x, ya, mc, gb, ga1, sc2, sh2, gpost, gpre, wao, wmo, wr, br)


def _rank_kernel(te_ref, rank_ref, cnt_ref, carry_ref):
    @pl.when(pl.program_id(0) == 0)
    def _():
        carry_ref[...] = jnp.zeros(carry_ref.shape, F32)

    te = te_ref[...]
    t = te.shape[0]
    lane = lax.broadcasted_iota(I32, (t, LANES), 1)
    hits = [lane == te[:, k:k + 1] for k in range(TOP_K_EXPERTS)]
    onehot = jnp.zeros((t, LANES), F32)
    for hit in hits:
        onehot = onehot + jnp.where(hit, 1.0, 0.0)
    lower = jnp.where(lax.broadcasted_iota(I32, (t, t), 0) > lax.broadcasted_iota(I32, (t, t), 1), 1.0, 0.0)
    before = jnp.dot(lower.astype(BF16), onehot.astype(BF16), preferred_element_type=F32) + carry_ref[...]
    rank = jnp.zeros((t, LANES), F32)
    for k, hit in enumerate(hits):
        rk = jnp.sum(jnp.where(hit, before, 0.0), axis=1, keepdims=True)
        rank = jnp.where(lane == k, rk, rank)
    rank_ref[...] = rank.astype(I32)
    carry_ref[...] = carry_ref[...] + jnp.sum(onehot, axis=0, keepdims=True)
    cnt_ref[...] = jnp.broadcast_to(carry_ref[...], cnt_ref.shape)


def _expert_ranks(te, *, rows):
    n = te.shape[0]
    return pl.pallas_call(
        _rank_kernel,
        grid=(n // rows,),
        in_specs=[pl.BlockSpec((rows, LANES), lambda i: (i, 0))],
        out_specs=[pl.BlockSpec((rows, LANES), lambda i: (i, 0)),
                   pl.BlockSpec((SUBLANES, LANES), lambda i: (0, 0))],
        out_shape=[jax.ShapeDtypeStruct((n, LANES), I32), jax.ShapeDtypeStruct((SUBLANES, LANES), F32)],
        scratch_shapes=[pltpu.VMEM((1, LANES), F32)],
        compiler_params=_params("arbitrary"),
        name="expert_ranks",
    )(te)


def _scatter_kernel(dest_ref, h2_ref, init_ref, xs_ref, sem, *, rows):
    del init_ref
    base = pl.program_id(0) * rows * TOP_K_EXPERTS

    def row_copy(n, k):
        return pltpu.make_async_copy(h2_ref.at[n], xs_ref.at[dest_ref[base + n * TOP_K_EXPERTS + k]], sem)

    def start(n, carry):
        for k in range(TOP_K_EXPERTS):
            row_copy(n, k).start()
        return carry

    def wait(n, carry):
        for k in range(TOP_K_EXPERTS):
            row_copy(n, k).wait()
        return carry

    lax.fori_loop(0, rows, start, 0)
    lax.fori_loop(0, rows, wait, 0)


def _scatter_rows(dest_flat, h2, cap, *, rows):
    n, s, _ = h2.shape
    init = jnp.zeros((cap, s, LANES), F32)
    grid_spec = pltpu.PrefetchScalarGridSpec(
        num_scalar_prefetch=1,
        grid=(n // rows,),
        in_specs=[pl.BlockSpec((rows, s, LANES), lambda i, d: (i, 0, 0)),
                  pl.BlockSpec(memory_space=pl.ANY)],
        out_specs=pl.BlockSpec(memory_space=pl.ANY),
        scratch_shapes=[pltpu.SemaphoreType.DMA(())],
    )
    return pl.pallas_call(
        functools.partial(_scatter_kernel, rows=rows),
        grid_spec=grid_spec,
        out_shape=jax.ShapeDtypeStruct((cap, s, LANES), F32),
        input_output_aliases={2: 0},
        compiler_params=_params("arbitrary"),
        name="scatter_rows",
    )(dest_flat, h2, init)


def _expert_kernel(be_ref, nu_ref, xs_ref, wgu_ref, bgu_ref, wdn_ref, bdn_ref, yb_ref):
    del be_ref

    @pl.when(pl.program_id(0) < nu_ref[0])
    def _():
        nseg = xs_ref.shape[1]
        x = jnp.concatenate([xs_ref[:, j, :] for j in range(nseg)], axis=1).astype(BF16)
        h = jnp.dot(x, wgu_ref[0], preferred_element_type=F32) + bgu_ref[0]
        gate = jnp.minimum(h[:, :D_FF], SWIGLU_LIMIT)
        up = jnp.clip(h[:, D_FF:], -SWIGLU_LIMIT, SWIGLU_LIMIT)
        act = (up + 1.0) * gate * jax.nn.sigmoid(SWIGLU_ALPHA * gate)
        y = jnp.dot(act.astype(BF16), wdn_ref[0], preferred_element_type=F32) + bdn_ref[0]
        for j in range(nseg):
            yb_ref[:, j, :] = y[:, j * LANES:(j + 1) * LANES]

    @pl.when(pl.program_id(0) >= nu_ref[0])
    def _():
        yb_ref[...] = jnp.zeros(yb_ref.shape, yb_ref.dtype)


def _expert_ffn(block_e, n_used, xs, w_gu, b_gu, w_dn, b_dn):
    cap, s, _ = xs.shape
    d = s * LANES
    nblk = cap // MOE_ROWS
    rows_map = lambda i, be, nu: (jnp.minimum(i, nu[0] - 1), 0, 0)
    by_expert = lambda i, be, nu: (be[i], 0, 0)
    grid_spec = pltpu.PrefetchScalarGridSpec(
        num_scalar_prefetch=2,
        grid=(nblk,),
        in_specs=[pl.BlockSpec((MOE_ROWS, s, LANES), rows_map),
                  pl.BlockSpec((1, d, 2 * D_FF), by_expert),
                  pl.BlockSpec((1, 1, 2 * D_FF), by_expert),
                  pl.BlockSpec((1, D_FF, d), by_expert),
                  pl.BlockSpec((1, 1, d), by_expert)],
        out_specs=pl.BlockSpec((MOE_ROWS, s, LANES), lambda i, be, nu: (i, 0, 0)),
    )
    return pl.pallas_call(
        _expert_kernel,
        grid_spec=grid_spec,
        out_shape=jax.ShapeDtypeStruct((cap, s, LANES), F32),
        compiler_params=_params("arbitrary"),
        name="expert_ffn",
    )(block_e, n_used, xs, w_gu, b_gu, w_dn, b_dn)


def _combine_kernel(dest_ref, gates_ref, x1_ref, ga2_ref, g_ref, yb_ref, o_ref, buf, sem, *, rows):
    nblk = pl.num_programs(1)
    step = pl.program_id(0) * nblk + pl.program_id(1)
    n_steps = pl.num_programs(0) * nblk

    def row_copy(st, slot, n, k):
        src = dest_ref[(st * rows + n) * TOP_K_EXPERTS + k]
        return pltpu.make_async_copy(yb_ref.at[src], buf.at[slot, k, n], sem.at[slot])

    def fetch(st, slot):
        def start(n, carry):
            for k in range(TOP_K_EXPERTS):
                row_copy(st, slot, n, k).start()
            return carry

        lax.fori_loop(0, rows, start, 0)

    @pl.when(step == 0)
    def _():
        fetch(0, 0)

    @pl.when(step + 1 < n_steps)
    def _():
        fetch(step + 1, (step + 1) % 2)

    slot = step % 2

    def wait(n, carry):
        for k in range(TOP_K_EXPERTS):
            row_copy(step, slot, n, k).wait()
        return carry

    lax.fori_loop(0, rows, wait, 0)
    gates = gates_ref[...]
    nseg = buf.shape[3]
    f = None
    for k in range(TOP_K_EXPERTS):
        yk = jnp.concatenate([buf[slot, k, :, j, :] for j in range(nseg)], axis=1)
        term = yk * gates[:, k:k + 1]
        f = term if f is None else f + term
    o_ref[0] = x1_ref[0] + ga2_ref[0] * _rms(f, g_ref[...])


def _combine(dest_flat, gates, x1, ga2, g, yb, *, rows):
    b, t, d = x1.shape
    nblk = t // rows
    per_row = ga2.shape[1] != 1
    row_blk = pl.BlockSpec((1, rows, d), lambda i, j, ds: (i, j, 0))
    mod_blk = row_blk if per_row else pl.BlockSpec((1, 1, d), lambda i, j, ds: (i, 0, 0))
    grid_spec = pltpu.PrefetchScalarGridSpec(
        num_scalar_prefetch=1,
        grid=(b, nblk),
        in_specs=[pl.BlockSpec((rows, LANES), lambda i, j, ds: (i * nblk + j, 0)),
                  row_blk, mod_blk,
                  pl.BlockSpec(g.shape, lambda i, j, ds: (0, 0)),
                  pl.BlockSpec(memory_space=pl.ANY)],
        out_specs=row_blk,
        scratch_shapes=[pltpu.VMEM((2, TOP_K_EXPERTS, rows, d // LANES, LANES), F32),
                        pltpu.SemaphoreType.DMA((2,))],
    )
    return pl.pallas_call(
        functools.partial(_combine_kernel, rows=rows),
        grid_spec=grid_spec,
        out_shape=jax.ShapeDtypeStruct((b, t, d), F32),
        compiler_params=_params("arbitrary", "arbitrary"),
        name="combine",
    )(dest_flat, gates, x1, ga2, g, yb)


def _moe(x1, h2, te, gates, ga2, g_post, w_gu, b_gu, w_dn, b_dn, *, rank_rows, io_rows):
    n = h2.shape[0]
    n_slots = n * TOP_K_EXPERTS
    rank, counts = _expert_ranks(te, rows=rank_rows)
    counts = counts[0, :N_EXPERTS].astype(I32)
    padded = (counts + MOE_ROWS - 1) // MOE_ROWS * MOE_ROWS
    pend = jnp.cumsum(padded)
    pstart = pend - padded
    nblk = (n_slots + N_EXPERTS * (MOE_ROWS - 1) + MOE_ROWS - 1) // MOE_ROWS
    n_used = (pend[-1] // MOE_ROWS).astype(I32)
    blk_start = jnp.arange(nblk, dtype=I32) * MOE_ROWS
    block_e = jnp.minimum(jnp.sum(pend[None, :] <= blk_start[:, None], axis=1), N_EXPERTS - 1).astype(I32)
    block_e = jnp.where(jnp.arange(nblk) < n_used, block_e, block_e[jnp.maximum(n_used - 1, 0)])
    top_e = te[:, :TOP_K_EXPERTS]
    dest = (pstart[top_e] + rank[:, :TOP_K_EXPERTS]).reshape(-1).astype(I32)
    xs = _scatter_rows(dest, h2, nblk * MOE_ROWS, rows=io_rows)
    yb = _expert_ffn(block_e, n_used.reshape(1), xs, w_gu, b_gu, w_dn, b_dn)
    return _combine(dest, gates, x1, ga2, g_post, yb, rows=io_rows)


def _t5_bucket(dist):
    n = jnp.maximum(dist, 0)
    max_exact = N_BUCKETS // 2
    nf = jnp.maximum(n, 1).astype(F32)
    large = max_exact + (jnp.log(nf / max_exact) / math.log(MAX_DISTANCE / max_exact)
                         * (N_BUCKETS - max_exact)).astype(I32)
    large = jnp.minimum(large, N_BUCKETS - 1)
    return jnp.where(n < max_exact, n, large)


def _split_in_proj(w_in):
    d = w_in.shape[0]
    wa = w_in[:, :OFF_KI].astype(BF16)
    wk = jnp.pad(w_in[:, OFF_KI:OFF_GA], ((0, 0), (0, LANES - (OFF_GA - OFF_KI)))).astype(BF16)
    wga = w_in[:, OFF_GA:OFF_GA + d].astype(BF16)
    wgb = w_in[:, OFF_GA + d:OFF_GA + 2 * d].astype(BF16)
    return wa, wk, wga, wgb


def kernel(x_prompt, x_sample, cache_k, cache_v, cache_kidx, state_conv, page_table, c_prompt, c_sample, rel_bias, w_mod, b_mod, g_pre_mix, g_post_mix, w_in, conv_w, w_conv_out, w_attn_out, w_mix_out, g_pre_ffn, g_post_ffn, w_router, b_router, w_gu, b_gu, w_dn, b_dn):
    depth = w_mod.shape[0]
    bp, seq, d = x_prompt.shape
    nb, dec_seq, _ = x_sample.shape
    assert dec_seq == 1
    page = cache_k.shape[2]
    past = page_table.shape[1] * page
    tq, tk = 256, 512
    rows_p = 256
    s_topk = min(TOPK_MAX, (past + dec_seq) // 4)

    tab = rel_bias.astype(F32)[_t5_bucket(jnp.arange(2 * LANES + 1, dtype=I32))].T
    rel = tab - tab[:, MAX_DISTANCE:MAX_DISTANCE + 1]
    ii = jnp.arange(LANES, dtype=I32)[:, None]
    jj = jnp.arange(LANES, dtype=I32)[None, :]
    bias_tiles = jnp.stack([rel[:, jnp.clip(ii - jj, 0, 2 * LANES)],
                            rel[:, jnp.clip(LANES + ii - jj, 0, 2 * LANES)]], axis=1)
    bias_s = tab[:, jnp.clip(past - jnp.arange(past + LANES, dtype=I32), 0, MAX_DISTANCE)]
    cache_k_t = jnp.transpose(cache_k, (0, 1, 3, 4, 2))
    cache_v_t = jnp.transpose(cache_v, (0, 1, 3, 4, 2))
    cache_kidx_t = jnp.transpose(cache_kidx, (0, 1, 3, 2))

    xp = x_prompt
    xs_rows = x_sample.reshape(1, nb, d)
    c_all = jnp.concatenate([c_prompt, c_sample], axis=0)
    n_c = c_all.shape[0]
    c_all = jnp.pad(c_all, ((0, -n_c % SUBLANES), (0, 0)))
    outs = [[] for _ in range(8)]
    for l in range(depth):
        mod = _modulation(c_all, w_mod[l].astype(BF16), b_mod[l])
        mod_p = [m[:, None, :] for m in jnp.split(mod[:bp], 6, axis=-1)]
        mod_s = [m[None] for m in jnp.split(mod[bp:bp + nb], 6, axis=-1)]
        wa, wk, wga, wgb = _split_in_proj(w_in[l])
        wts = (wa, wk, wga, wgb, conv_w[l], w_conv_out[l].astype(BF16))
        g1 = g_pre_mix[l].reshape(1, d)
        wao = w_attn_out[l].astype(BF16)
        wmo = w_mix_out[l].astype(BF16)
        wr = jnp.pad(w_router[l], ((0, 0), (0, LANES - N_EXPERTS))).astype(BF16)
        br = jnp.pad(b_router[l], (0, LANES - N_EXPERTS)).reshape(1, LANES)
        gpost = g_post_mix[l].reshape(1, d)
        gpre2 = g_pre_ffn[l].reshape(1, d)
        gpost2 = g_post_ffn[l].reshape(1, d)
        wgu = w_gu[l].astype(BF16)
        bgu = b_gu[l].reshape(N_EXPERTS, 1, 2 * D_FF)
        wdn = w_dn[l].astype(BF16)
        bdn = b_dn[l].reshape(N_EXPERTS, 1, d)

        st = state_conv[l].reshape(1, nb, 2 * D_CONV)
        q, k, v, qi, kiwi, mc, gb, u = _in_projection(
            xs_rows, mod_s[1], mod_s[0], st, g1, wts, seq_conv=False, rows=nb)
        kidx = kiwi[0, :, :IDX_DIM]
        wi = kiwi[0, :, IDX_DIM:IDX_DIM + N_IDX_HEADS]
        scores = _sample_scores(page_table, qi.reshape(nb, N_IDX_HEADS, IDX_DIM), wi[:, :, None],
                                kidx[:, None, :], cache_kidx_t, l)
        madd = _sample_select(scores.reshape(nb, past + LANES), s_topk)
        k3 = k.reshape(nb, N_HEADS, HEAD_DIM)
        v3 = v.reshape(nb, N_HEADS, HEAD_DIM)
        ya = _sample_attention(page_table, q.astype(F32).reshape(nb, N_HEADS, HEAD_DIM), k3, v3,
                               madd.reshape(nb, 1, past + LANES), bias_s, cache_k_t, cache_v_t, l,
                               pages_per_step=16)
        ya = jnp.swapaxes(ya[:, :, :N_HEADS], 1, 2)
        x1, h2, te, gates = _mix_and_route(xs_rows, ya.reshape(1, nb, D_ATTN), mc, gb, mod_s[2], mod_s[4],
                                           mod_s[3], gpost, gpre2, wao, wmo, wr, br, rows=nb)
        xs_rows = _moe(x1, h2, te, gates, mod_s[5], gpost2, wgu, bgu, wdn, bdn, rank_rows=nb, io_rows=nb)
        outs[4].append(k.reshape(nb, dec_seq, N_HEADS, HEAD_DIM))
        outs[5].append(v.reshape(nb, dec_seq, N_HEADS, HEAD_DIM))
        outs[6].append(kidx[:, None, :])
        outs[7].append(jnp.stack([state_conv[l][:, 1, :], u[0]], axis=1))

        prefix = jnp.zeros((bp, 1, 2 * D_CONV), F32)
        q, k, v, qi, kiwi, mc, gb, u_tail = _in_projection(
            xp, mod_p[1], mod_p[0], prefix, g1, wts, seq_conv=True, rows=rows_p)
        kidx = kiwi[..., :IDX_DIM]
        by_head = lambda a: jnp.swapaxes(a.reshape(bp, seq, N_HEADS, HEAD_DIM), 1, 2)
        ya = _prompt_attention(by_head(q), by_head(qi), kiwi, jnp.swapaxes(kidx, 1, 2).astype(BF16),
                               (jnp.swapaxes(k, 1, 2) * LOG2E).astype(BF16), v.astype(BF16),
                               bias_tiles * LOG2E, tq=tq, tk=tk)
        x1, h2, te, gates = _mix_and_route(xp, ya, mc, gb, mod_p[2], mod_p[4], mod_p[3], gpost, gpre2,
                                           wao, wmo, wr, br, rows=rows_p)
        xp = _moe(x1, h2, te, gates, mod_p[5], gpost2, wgu, bgu, wdn, bdn, rank_rows=512, io_rows=rows_p)
        outs[0].append(k.reshape(bp, seq, N_HEADS, HEAD_DIM))
        outs[1].append(v.reshape(bp, seq, N_HEADS, HEAD_DIM))
        outs[2].append(kidx)
        outs[3].append(u_tail[:, SUBLANES - (CONV_W - 1):, :])
    return (xp, xs_rows.reshape(nb, dec_seq, d)) + tuple(jnp.stack(o) for o in outs)
```

```python
import functools
import math

import jax
import jax.numpy as jnp
from jax import lax
from jax.experimental import pallas as pl
from jax.experimental.pallas import tpu as pltpu

F32 = jnp.float32
BF16 = jnp.bfloat16
I32 = jnp.int32

D_CONV = 512
CONV_W = 3
N_HEADS = 8
HEAD_DIM = 64
D_ATTN = N_HEADS * HEAD_DIM
ATTN_SCALE = HEAD_DIM ** -0.5
N_IDX_HEADS = 8
IDX_DIM = 64
INDEX_SCALE = (N_IDX_HEADS * IDX_DIM) ** -0.5
TOPK_MAX = 256
N_BUCKETS = 32
MAX_DISTANCE = 128
N_EXPERTS = 32
TOP_K_EXPERTS = 4
D_FF = 1024
SWIGLU_LIMIT = 7.0
SWIGLU_ALPHA = 1.702
EPS = 1e-6

LANES = 128
SUBLANES = 8
VMEM_LIMIT_BYTES = 56 * 1024 * 1024

OFF_Q = 3 * D_CONV
OFF_K = OFF_Q + D_ATTN
OFF_V = OFF_K + D_ATTN
OFF_QI = OFF_V + D_ATTN
OFF_KI = OFF_QI + N_IDX_HEADS * IDX_DIM
OFF_WI = OFF_KI + IDX_DIM
OFF_GA = OFF_WI + N_IDX_HEADS

LOG2E = 1.4426950408889634
INT_MIN = -2 ** 31
NEG_BIG = -1e30
MOE_ROWS = 512


def _params(*sem):
    return pltpu.CompilerParams(dimension_semantics=sem, vmem_limit_bytes=VMEM_LIMIT_BYTES)


def _const_spec(shape):
    zeros = (0,) * len(shape)
    return pl.BlockSpec(shape, lambda *_: zeros, pipeline_mode=pl.Buffered(1))


def _rms(x, g):
    ms = jnp.mean(x * x, axis=-1, keepdims=True)
    return (x * lax.rsqrt(ms + EPS)) * g


def _sort_key(x):
    bits = pltpu.bitcast(x, I32)
    return jnp.where(bits < 0, bits ^ jnp.int32(0x7FFFFFFF), bits)


def _mod_kernel(c_ref, w_ref, b_ref, o_ref):
    c = c_ref[...]
    s = c * jax.nn.sigmoid(c)
    o_ref[...] = jnp.dot(s.astype(BF16), w_ref[...], preferred_element_type=F32) + b_ref[...]


def _modulation(c, w_mod, b_mod):
    n, d = c.shape
    n_out = w_mod.shape[1]
    bn = 1024
    return pl.pallas_call(
        _mod_kernel,
        grid=(n_out // bn,),
        in_specs=[pl.BlockSpec((n, d), lambda j: (0, 0)),
                  pl.BlockSpec((d, bn), lambda j: (0, j)),
                  pl.BlockSpec((1, bn), lambda j: (0, j))],
        out_specs=pl.BlockSpec((n, bn), lambda j: (0, j)),
        out_shape=jax.ShapeDtypeStruct((n, n_out), F32),
        compiler_params=_params("arbitrary"),
        name="modulation",
    )(c, w_mod, b_mod.reshape(1, n_out))


def _inproj_kernel(x_ref, sc_ref, sh_ref, st_ref, g_ref, wa_ref, wk_ref, wga_ref, wgb_ref, cw_ref, wco_ref,
                   q_ref, k_ref, v_ref, qi_ref, kiwi_ref, mc_ref, gb_ref, u_ref, carry_ref, *, seq_conv):
    x = x_ref[0]
    t = x.shape[0]
    h = _rms(x, g_ref[...]) * (1.0 + sc_ref[0]) + sh_ref[0]
    hb = h.astype(BF16)

    def proj(lo, hi):
        return jnp.dot(hb, wa_ref[:, lo:hi], preferred_element_type=F32)

    b_gate = proj(0, D_CONV)
    u = proj(D_CONV, 2 * D_CONV) * proj(2 * D_CONV, 3 * D_CONV)
    if seq_conv:
        @pl.when(pl.program_id(1) == 0)
        def _():
            carry_ref[0:1, :] = st_ref[0][:, 0:D_CONV]
            carry_ref[1:2, :] = st_ref[0][:, D_CONV:2 * D_CONV]

        cm2 = carry_ref[0:1, :]
        cm1 = carry_ref[1:2, :]
        rows = lax.broadcasted_iota(I32, u.shape, 0)
        prev1 = jnp.where(rows == 0, cm1, pltpu.roll(u, 1, 0))
        prev2 = jnp.where(rows == 0, cm2, jnp.where(rows == 1, cm1, pltpu.roll(u, 2, 0)))
        carry_ref[...] = u[t - 2:t, :]
        u_ref[0] = u[t - SUBLANES:t, :]
    else:
        prev2 = st_ref[0][:, 0:D_CONV]
        prev1 = st_ref[0][:, D_CONV:2 * D_CONV]
        u_ref[0] = u
    cw = cw_ref[...]
    y_conv = b_gate * (cw[0:1, :] * prev2 + cw[1:2, :] * prev1 + cw[2:3, :] * u)
    g_a = jax.nn.sigmoid(jnp.dot(hb, wga_ref[...], preferred_element_type=F32))
    mc_ref[0] = g_a * jnp.dot(y_conv.astype(BF16), wco_ref[...], preferred_element_type=F32)
    gb_ref[0] = jax.nn.sigmoid(jnp.dot(hb, wgb_ref[...], preferred_element_type=F32))
    q_ref[0] = (proj(OFF_Q, OFF_K) * ATTN_SCALE).astype(BF16)
    k_ref[0] = proj(OFF_K, OFF_V)
    v_ref[0] = proj(OFF_V, OFF_QI)
    qi_ref[0] = proj(OFF_QI, OFF_KI).astype(BF16)
    kiwi_ref[0] = jnp.dot(hb, wk_ref[...], preferred_element_type=F32)


def _in_projection(x, sc, sh, st, g, wts, *, seq_conv, rows):
    b, t, d = x.shape
    nblk = t // rows
    tm = 1 if sc.shape[1] == 1 else rows
    ts = 1 if seq_conv else rows
    tu = SUBLANES if seq_conv else rows
    wa, wk, wga, wgb, cw, wco = wts

    def row_spec(width, per_row):
        if per_row:
            return pl.BlockSpec((1, rows, width), lambda i, j: (i, j, 0))
        return pl.BlockSpec((1, 1, width), lambda i, j: (i, 0, 0))

    out_widths = (D_ATTN, D_ATTN, D_ATTN, N_IDX_HEADS * IDX_DIM, LANES, d, d)
    out_dtypes = (BF16, F32, F32, BF16, F32, F32, F32)
    out_shape = [jax.ShapeDtypeStruct((b, t, w), dt) for w, dt in zip(out_widths, out_dtypes)]
    out_specs = [row_spec(w, True) for w in out_widths]
    if seq_conv:
        out_shape.append(jax.ShapeDtypeStruct((b, tu, D_CONV), F32))
        out_specs.append(pl.BlockSpec((1, tu, D_CONV), lambda i, j: (i, 0, 0)))
    else:
        out_shape.append(jax.ShapeDtypeStruct((b, t, D_CONV), F32))
        out_specs.append(row_spec(D_CONV, True))
    return pl.pallas_call(
        functools.partial(_inproj_kernel, seq_conv=seq_conv),
        grid=(b, nblk),
        in_specs=[row_spec(d, True), row_spec(d, tm != 1), row_spec(d, tm != 1),
                  row_spec(2 * D_CONV, ts != 1),
                  _const_spec(g.shape), _const_spec(wa.shape), _const_spec(wk.shape),
                  _const_spec(wga.shape), _const_spec(wgb.shape), _const_spec(cw.shape),
                  _const_spec(wco.shape)],
        out_specs=out_specs,
        out_shape=out_shape,
        scratch_shapes=[pltpu.VMEM((2, D_CONV), F32)],
        compiler_params=_params("arbitrary", "arbitrary"),
        name="in_projection",
    )(x, sc, sh, st, g, wa, wk, wga, wgb, cw, wco)


ROW_TILE = 128
ATT_ROWS = 256


def _lane_rep(col, width=LANES):
    return jnp.broadcast_to(col, (col.shape[0], width))


def _tile_lanes(x, n):
    return x if n == 1 else jnp.concatenate([x] * n, axis=1)


def _count_rows(keys_ref, nblk, tq, tk, pred_of_tile):
    out = []
    for r in range(tq // ROW_TILE):
        rows = slice(r * ROW_TILE, (r + 1) * ROW_TILE)
        pred = pred_of_tile(r)

        def body(kb, acc, rows=rows, pred=pred):
            off = pl.multiple_of(kb * tk, tk)
            for c in range(tk // LANES):
                blk = keys_ref[rows, pl.ds(off + c * LANES, LANES)]
                acc = acc + jnp.where(pred(blk), 1.0, 0.0)
            return acc

        out.append(lax.fori_loop(0, nblk, body, jnp.zeros((ROW_TILE, LANES), F32)))
    assert keys_ref.shape[1] // LANES < 256
    partial = jnp.concatenate(out, axis=0).astype(BF16)
    return jnp.dot(partial, jnp.ones((LANES, LANES), BF16), preferred_element_type=F32).astype(I32)


def _kth_largest_key(keys_ref, nblk, tq, tk, k):
    def bit_body(i, kk):
        cand = kk + (jnp.int32(1) << (31 - i))

        def pred_of_tile(r):
            cb = cand[r * ROW_TILE:(r + 1) * ROW_TILE]
            return lambda blk: blk >= cb

        cnt = _count_rows(keys_ref, nblk, tq, tk, pred_of_tile)
        return jnp.where(cnt >= k, cand, kk)

    return lax.fori_loop(0, 32, bit_body, jnp.full((tq, LANES), INT_MIN, I32))


def _pattn_kernel(q_ref, qi_ref, kiwi_ref, kidxt_ref, kt_ref, v_ref, bias_ref, upper_ref, o_ref,
                  keys_ref, wb_ref, madd_ref, m_ref, acc_ref, eqc_ref, *, tq, tk, topk):
    qb = pl.program_id(1)
    kd = (qb * tq) // tk
    nkb = kd + 1
    reps = tk // LANES
    wi = kiwi_ref[0][:, IDX_DIM:IDX_DIM + N_IDX_HEADS]
    for h in range(N_IDX_HEADS):
        wb_ref[h] = _lane_rep(wi[:, h:h + 1])

    col_w = 2 * LANES
    row_i = lax.broadcasted_iota(I32, (ROW_TILE, col_w), 0)
    col_i = lax.broadcasted_iota(I32, (ROW_TILE, col_w), 1)

    def score_block(kb, carry):
        off = pl.multiple_of(kb * tk, tk)
        for r in range(tq // ROW_TILE):
            rows = slice(r * ROW_TILE, (r + 1) * ROW_TILE)
            for c in range(tk // col_w):
                kx = kidxt_ref[0, :, pl.ds(off + c * col_w, col_w)]
                acc = jnp.zeros((ROW_TILE, col_w), F32)
                for h in range(N_IDX_HEADS):
                    s = jnp.dot(qi_ref[0, h, rows, :], kx, preferred_element_type=F32)
                    acc = acc + jnp.maximum(s, 0.0) * _tile_lanes(wb_ref[h, rows, :], col_w // LANES)
                key = _sort_key(acc * INDEX_SCALE)
                valid = (off + c * col_w + col_i) <= (qb * tq + r * ROW_TILE + row_i)
                keys_ref[rows, pl.ds(off + c * col_w, col_w)] = jnp.where(valid, key, INT_MIN)
        return carry

    lax.fori_loop(0, nkb, score_block, 0)

    kth = _kth_largest_key(keys_ref, nkb, tq, tk, topk)

    def gt_of_tile(r):
        kb_ = kth[r * ROW_TILE:(r + 1) * ROW_TILE]
        return lambda blk: blk > kb_

    cnt_gt = _count_rows(keys_ref, nkb, tq, tk, gt_of_tile)
    need = jnp.where(kth == INT_MIN, 0, topk - cnt_gt).astype(F32)
    kth_w = _tile_lanes(kth, reps)
    need_w = _tile_lanes(need, reps)

    m_ref[...] = jnp.full(m_ref.shape, NEG_BIG, F32)
    acc_ref[...] = jnp.zeros(acc_ref.shape, F32)
    eqc_ref[...] = jnp.zeros(eqc_ref.shape, F32)

    def near_bias(h, kb, r):
        nrow = min(ATT_ROWS, LANES)
        strips = []
        for r0 in range(r * ATT_ROWS, (r + 1) * ATT_ROWS, nrow):
            i, sub = divmod(r0, LANES)
            t0 = bias_ref[h, 0, sub:sub + nrow, :]
            t1 = bias_ref[h, 1, sub:sub + nrow, :]
            tiles = []
            for j in range(reps):
                d = qb * (tq // LANES) + i - kb * reps - j
                tiles.append(jnp.where(d == 0, t0, jnp.where(d == 1, t1, 0.0)))
            strips.append(jnp.concatenate(tiles, axis=1))
        return strips[0] if len(strips) == 1 else jnp.concatenate(strips, axis=0)

    lane_k = lax.broadcasted_iota(I32, (tk, LANES), 1)
    own = [jnp.where((lane_k < HEAD_DIM) == (par == 0), 1.0, 0.0).astype(BF16) for par in (0, 1)]
    other = [own[1], own[0]]

    def attend_rows(off, rows, bias_of_head):
        madd = madd_ref[rows, :]
        for h in range(N_HEADS):
            s = jnp.dot(q_ref[0, h, rows, :], kt_ref[0, h * HEAD_DIM:(h + 1) * HEAD_DIM, pl.ds(off, tk)],
                        preferred_element_type=F32) + madd
            if bias_of_head is not None:
                s = s + bias_of_head(h)
            m_old = m_ref[h, rows, :]
            m_new = jnp.maximum(m_old, _lane_rep(jnp.max(s, axis=1, keepdims=True)))
            alpha = jnp.exp2(m_old - m_new)
            p = jnp.exp2(s - _tile_lanes(m_new, reps))
            pair = slice((h // 2) * LANES, (h // 2 + 1) * LANES)
            v_aug = v_ref[0, pl.ds(off, tk), pair] * own[h % 2] + other[h % 2]
            pv = jnp.dot(p.astype(BF16), v_aug, preferred_element_type=F32)
            acc_ref[h, rows, :] = alpha * acc_ref[h, rows, :] + pv
            m_ref[h, rows, :] = m_new

    def attend_block(kb, near):
        off = pl.multiple_of(kb * tk, tk)
        sk = keys_ref[:, pl.ds(off, tk)]
        eq = sk == kth_w
        eqf = jnp.where(eq, 1.0, 0.0)
        rank = jnp.dot(eqf.astype(BF16), upper_ref[...], preferred_element_type=F32)
        rank = rank + _tile_lanes(eqc_ref[...], reps)
        eqc_ref[...] = eqc_ref[...] + _lane_rep(jnp.sum(eqf, axis=1, keepdims=True))
        madd_ref[...] = jnp.where(sk > kth_w, 0.0, jnp.where(eq, jnp.where(rank < need_w, 0.0, NEG_BIG), NEG_BIG))
        for r in range(tq // ATT_ROWS):
            bias_of_head = (lambda h, r=r: near_bias(h, kb, r)) if near else None
            attend_rows(off, slice(r * ATT_ROWS, (r + 1) * ATT_ROWS), bias_of_head)

    def far_block(kb, carry):
        attend_block(kb, False)
        return carry

    lax.fori_loop(0, jnp.maximum(kd - 1, 0), far_block, 0)

    @pl.when(kd >= 1)
    def _():
        attend_block(kd - 1, True)

    attend_block(kd, True)
    lane_q = lax.broadcasted_iota(I32, (tq, LANES), 1)
    for j in range(N_HEADS // 2):
        even = acc_ref[2 * j]
        odd = acc_ref[2 * j + 1]
        out = jnp.where(lane_q < HEAD_DIM, even / pltpu.roll(even, HEAD_DIM, 1), odd / pltpu.roll(odd, HEAD_DIM, 1))
        o_ref[0, :, j * LANES:(j + 1) * LANES] = out.astype(o_ref.dtype)


def _prompt_attention(q, qi, kiwi, kidxt, kt, v, bias_tiles, *, tq, tk):
    b, _, s, _ = q.shape
    topk = min(TOPK_MAX, s // 4)
    assert tk % (2 * LANES) == 0 and tk % tq == 0 and tq % ROW_TILE == 0 and s % tk == 0
    assert LANES >= MAX_DISTANCE and tk >= 2 * LANES
    upper = jnp.triu(jnp.ones((tk, tk), BF16), 1)
    heads = lambda: pl.BlockSpec((1, N_HEADS, tq, HEAD_DIM), lambda i, j: (i, 0, j, 0))
    per_b = lambda r, c: pl.BlockSpec((1, r, c), lambda i, j: (i, 0, 0), pipeline_mode=pl.Buffered(1))
    stat = lambda w: pltpu.VMEM((N_HEADS, tq, w), F32)
    return pl.pallas_call(
        functools.partial(_pattn_kernel, tq=tq, tk=tk, topk=topk),
        grid=(b, s // tq),
        in_specs=[heads(), heads(), pl.BlockSpec((1, tq, LANES), lambda i, j: (i, j, 0)),
                  per_b(IDX_DIM, s), per_b(D_ATTN, s), per_b(s, D_ATTN),
                  _const_spec(bias_tiles.shape), _const_spec(upper.shape)],
        out_specs=pl.BlockSpec((1, tq, D_ATTN), lambda i, j: (i, j, 0)),
        out_shape=jax.ShapeDtypeStruct((b, s, D_ATTN), BF16),
        scratch_shapes=[pltpu.VMEM((tq, s), I32), stat(LANES), pltpu.VMEM((tq, tk), F32),
                        stat(LANES), stat(LANES), pltpu.VMEM((tq, LANES), F32)],
        compiler_params=_params("arbitrary", "arbitrary"),
        name="prompt_attention",
    )(q, qi, kiwi, kidxt, kt, v, bias_tiles, upper)


def _sscore_kernel(pt_ref, qi_ref, wi_ref, kn_ref, cache_ref, o_ref, buf, sem, *, layer, n_pages, page):
    b = pl.program_id(0)
    nb = pl.num_programs(0)
    past = n_pages * page

    def page_copy(bb, slot, p):
        return pltpu.make_async_copy(cache_ref.at[layer, pt_ref[bb, p]], buf.at[slot, p], sem.at[slot])

    def fetch(bb, slot):
        for p in range(n_pages):
            page_copy(bb, slot, p).start()

    @pl.when(b == 0)
    def _():
        fetch(0, 0)

    @pl.when(b + 1 < nb)
    def _():
        fetch(b + 1, (b + 1) % 2)

    slot = b % 2
    for p in range(n_pages):
        page_copy(b, slot, p).wait()
    qi = qi_ref[0]
    wi = wi_ref[0]
    group = 8
    for g in range(n_pages // group):
        kx = jnp.concatenate([buf[slot, g * group + j] for j in range(group)], axis=1).astype(BF16)
        s = jnp.dot(qi, kx, preferred_element_type=F32)
        sc = jnp.sum(jnp.maximum(s, 0.0) * wi, axis=0, keepdims=True) * INDEX_SCALE
        o_ref[0, :, g * group * page:(g + 1) * group * page] = sc
    kn = kn_ref[0].astype(BF16).astype(F32)
    sn = jnp.sum(qi.astype(F32) * kn, axis=1, keepdims=True)
    s_new = jnp.sum(jnp.maximum(sn, 0.0) * wi, axis=0, keepdims=True) * INDEX_SCALE
    lane = lax.broadcasted_iota(I32, (1, LANES), 1)
    o_ref[0, :, past:past + LANES] = jnp.where(lane == 0, s_new, -jnp.inf)


def _sample_scores(page_table, qi, wi, kidx_new, cache_kidx_t, layer):
    nb, n_pages = page_table.shape
    page = cache_kidx_t.shape[3]
    past = n_pages * page
    assert n_pages % 8 == 0 and page == LANES
    grid_spec = pltpu.PrefetchScalarGridSpec(
        num_scalar_prefetch=1,
        grid=(nb,),
        in_specs=[pl.BlockSpec((1, N_IDX_HEADS, IDX_DIM), lambda i, pt: (i, 0, 0)),
                  pl.BlockSpec((1, N_IDX_HEADS, 1), lambda i, pt: (i, 0, 0)),
                  pl.BlockSpec((1, 1, IDX_DIM), lambda i, pt: (i, 0, 0)),
                  pl.BlockSpec(memory_space=pl.ANY)],
        out_specs=pl.BlockSpec((1, 1, past + LANES), lambda i, pt: (i, 0, 0)),
        scratch_shapes=[pltpu.VMEM((2, n_pages, IDX_DIM, page), F32), pltpu.SemaphoreType.DMA((2,))],
    )
    return pl.pallas_call(
        functools.partial(_sscore_kernel, layer=layer, n_pages=n_pages, page=page),
        grid_spec=grid_spec,
        out_shape=jax.ShapeDtypeStruct((nb, 1, past + LANES), F32),
        compiler_params=_params("arbitrary"),
        name="sample_scores",
    )(page_table, qi, wi, kidx_new, cache_kidx_t)


def _ssel_kernel(sc_ref, madd_ref, keys_ref, *, topk):
    nb, width = sc_ref.shape
    nblk = width // LANES
    keys_ref[...] = _sort_key(sc_ref[...])

    def count(pred):
        def body(c, acc):
            off = pl.multiple_of(c * LANES, LANES)
            return acc + jnp.where(pred(keys_ref[:, pl.ds(off, LANES)]), 1, 0)
        acc = lax.fori_loop(0, nblk, body, jnp.zeros((nb, LANES), I32))
        return jnp.sum(acc, axis=1, keepdims=True)

    def bit_body(i, kk):
        cand = kk + (jnp.int32(1) << (31 - i))
        return jnp.where(count(lambda blk: blk >= cand) >= topk, cand, kk)

    kth = lax.fori_loop(0, 32, bit_body, jnp.full((nb, 1), INT_MIN, I32))
    need = (topk - count(lambda blk: blk > kth)).astype(F32)
    r_i = lax.broadcasted_iota(I32, (LANES, LANES), 0)
    c_i = lax.broadcasted_iota(I32, (LANES, LANES), 1)
    upper = jnp.where(r_i < c_i, 1.0, 0.0).astype(BF16)

    def mask_block(c, eqc):
        off = pl.multiple_of(c * LANES, LANES)
        sk = keys_ref[:, pl.ds(off, LANES)]
        eq = sk == kth
        eqf = jnp.where(eq, 1.0, 0.0)
        rank = jnp.dot(eqf.astype(BF16), upper, preferred_element_type=F32) + eqc
        madd_ref[:, pl.ds(off, LANES)] = jnp.where(
            sk > kth, 0.0, jnp.where(eq, jnp.where(rank < need, 0.0, NEG_BIG), NEG_BIG))
        return eqc + jnp.sum(eqf, axis=1, keepdims=True)

    lax.fori_loop(0, nblk, mask_block, jnp.zeros((nb, 1), F32))


def _sample_select(scores, topk):
    nb, width = scores.shape
    return pl.pallas_call(
        functools.partial(_ssel_kernel, topk=topk),
        out_shape=jax.ShapeDtypeStruct((nb, width), F32),
        scratch_shapes=[pltpu.VMEM((nb, width), I32)],
        compiler_params=pltpu.CompilerParams(vmem_limit_bytes=VMEM_LIMIT_BYTES),
        name="sample_select",
    )(scores)


def _sattn_kernel(pt_ref, q_ref, qt_ref, kn_ref, vnt_ref, madd_ref, bias_ref, ck_ref, cv_ref, o_ref,
                  kbuf, vbuf, qb_ref, acc_ref, m_ref, l_ref, sem, *, layer, cp, page, past):
    b = pl.program_id(0)
    c = pl.program_id(1)
    nc = pl.num_programs(1)
    step = b * nc + c
    n_steps = pl.num_programs(0) * nc

    def page_copies(bb, cc, slot, j):
        phys = pt_ref[bb, cc * cp + j]
        return (pltpu.make_async_copy(ck_ref.at[layer, phys], kbuf.at[slot, j], sem.at[0, slot]),
                pltpu.make_async_copy(cv_ref.at[layer, phys], vbuf.at[slot, j], sem.at[1, slot]))

    def fetch(bb, cc, slot):
        for j in range(cp):
            ck, cv = page_copies(bb, cc, slot, j)
            ck.start()
            cv.start()

    @pl.when(step == 0)
    def _():
        fetch(0, 0, 0)

    nxt = step + 1

    @pl.when(nxt < n_steps)
    def _():
        fetch(nxt // nc, nxt % nc, nxt % 2)

    slot = step % 2
    for j in range(cp):
        ck, cv = page_copies(b, c, slot, j)
        ck.wait()
        cv.wait()

    @pl.when(c == 0)
    def _():
        m_ref[...] = jnp.full(m_ref.shape, NEG_BIG, F32)
        l_ref[...] = jnp.zeros(l_ref.shape, F32)
        acc_ref[...] = jnp.zeros(acc_ref.shape, F32)
        qt = qt_ref[0]
        for h in range(N_HEADS):
            qb_ref[h] = _lane_rep(qt[:, h:h + 1], page)

    pages = []
    for j in range(cp):
        rows = [jnp.sum(kbuf[slot, j, h] * qb_ref[h], axis=0, keepdims=True) for h in range(N_HEADS)]
        pages.append(jnp.concatenate(rows, axis=0))
    width = cp * page
    col0 = pl.multiple_of(c * width, width)
    s = jnp.concatenate(pages, axis=1) + bias_ref[:, pl.ds(col0, width)] + madd_ref[0, :, pl.ds(col0, width)]
    m_old = m_ref[...]
    m_new = jnp.maximum(m_old, _lane_rep(jnp.max(s, axis=1, keepdims=True)))
    alpha = jnp.exp(m_old - m_new)
    p = jnp.exp(s - m_new[:, 0:1])
    l_ref[...] = alpha * l_ref[...] + _lane_rep(jnp.sum(p, axis=1, keepdims=True))
    m_ref[...] = m_new
    for h in range(N_HEADS):
        acc = acc_ref[h] * jnp.broadcast_to(alpha[h:h + 1, :], (HEAD_DIM, page))
        for j in range(cp):
            acc = acc + vbuf[slot, j, h] * jnp.broadcast_to(p[h:h + 1, j * page:(j + 1) * page], (HEAD_DIM, page))
        acc_ref[h] = acc

    @pl.when(c == nc - 1)
    def _():
        s_new = (jnp.sum(q_ref[0] * kn_ref[0], axis=1, keepdims=True)
                 + bias_ref[:, past:past + 1] + madd_ref[0, :, past:past + 1])
        m_prev = m_ref[...]
        m_fin = jnp.maximum(m_prev, s_new)
        a_fin = jnp.exp(m_prev - m_fin)
        p_new = jnp.exp(s_new - m_fin)
        l_fin = a_fin * l_ref[...] + p_new
        lane = lax.broadcasted_iota(I32, (HEAD_DIM, LANES), 1)
        out = jnp.zeros((HEAD_DIM, LANES), F32)
        vnt = vnt_ref[0]
        for h in range(N_HEADS):
            tot = jnp.sum(acc_ref[h], axis=1, keepdims=True)
            col = (tot * a_fin[h:h + 1, 0:1] + p_new[h:h + 1, 0:1] * vnt[:, h:h + 1]) / l_fin[h:h + 1, 0:1]
            out = jnp.where(lane == h, col, out)
        o_ref[0] = out


def _sample_attention(page_table, q, k_new, v_new, madd, bias_s, cache_k_t, cache_v_t, layer, *, pages_per_step):
    nb, n_pages = page_table.shape
    page = cache_k_t.shape[4]
    past = n_pages * page
    cp = pages_per_step
    assert n_pages % cp == 0 and page == LANES
    head_blk = pl.BlockSpec((1, N_HEADS, HEAD_DIM), lambda i, j, pt: (i, 0, 0))
    tr_blk = pl.BlockSpec((1, HEAD_DIM, N_HEADS), lambda i, j, pt: (i, 0, 0))
    grid_spec = pltpu.PrefetchScalarGridSpec(
        num_scalar_prefetch=1,
        grid=(nb, n_pages // cp),
        in_specs=[head_blk, tr_blk, head_blk, tr_blk,
                  pl.BlockSpec((1, 1, past + LANES), lambda i, j, pt: (i, 0, 0)),
                  pl.BlockSpec(bias_s.shape, lambda i, j, pt: (0, 0)),
                  pl.BlockSpec(memory_space=pl.ANY), pl.BlockSpec(memory_space=pl.ANY)],
        out_specs=pl.BlockSpec((1, HEAD_DIM, LANES), lambda i, j, pt: (i, 0, 0)),
        scratch_shapes=[pltpu.VMEM((2, cp, N_HEADS, HEAD_DIM, page), F32),
                        pltpu.VMEM((2, cp, N_HEADS, HEAD_DIM, page), F32),
                        pltpu.VMEM((N_HEADS, HEAD_DIM, page), F32), pltpu.VMEM((N_HEADS, HEAD_DIM, page), F32),
                        pltpu.VMEM((N_HEADS, LANES), F32), pltpu.VMEM((N_HEADS, LANES), F32),
                        pltpu.SemaphoreType.DMA((2, 2))],
    )
    return pl.pallas_call(
        functools.partial(_sattn_kernel, layer=layer, cp=cp, page=page, past=past),
        grid_spec=grid_spec,
        out_shape=jax.ShapeDtypeStruct((nb, HEAD_DIM, LANES), F32),
        compiler_params=_params("arbitrary", "arbitrary"),
        name="sample_attention",
    )(page_table, q, jnp.swapaxes(q, 1, 2), k_new, jnp.swapaxes(v_new, 1, 2), madd, bias_s, cache_k_t, cache_v_t)


def _mix_kernel(x_ref, ya_ref, mc_ref, gb_ref, ga1_ref, sc2_ref, sh2_ref, gpost_ref, gpre_ref,
                wao_ref, wmo_ref, wr_ref, br_ref, x1_ref, h2_ref, te_ref, gates_ref):
    x = x_ref[0]
    t, d = x.shape
    attn = jnp.dot(ya_ref[0].astype(BF16), wao_ref[...], preferred_element_type=F32)
    merged = mc_ref[0] + gb_ref[0] * attn
    z = jnp.dot(merged.astype(BF16), wmo_ref[...], preferred_element_type=F32)
    x1 = x + ga1_ref[0] * _rms(z, gpost_ref[...])
    x1_ref[0] = x1
    h2 = _rms(x1, gpre_ref[...]) * (1.0 + sc2_ref[0]) + sh2_ref[0]
    for j in range(d // LANES):
        h2_ref[:, j, :] = h2[:, j * LANES:(j + 1) * LANES]
    lane = lax.broadcasted_iota(I32, (t, LANES), 1)
    logits = jnp.dot(h2.astype(BF16), wr_ref[...], preferred_element_type=F32) + br_ref[...]
    work = jnp.where(lane < N_EXPERTS, logits, -jnp.inf)
    top_l, top_e = [], []
    for _ in range(TOP_K_EXPERTS):
        mk = jnp.max(work, axis=1, keepdims=True)
        ek = jnp.min(jnp.where(work == mk, lane, LANES), axis=1, keepdims=True)
        top_l.append(mk)
        top_e.append(ek)
        work = jnp.where(lane == ek, -jnp.inf, work)
    ex = [jnp.exp(tl - top_l[0]) for tl in top_l]
    denom = ex[0] + ex[1] + ex[2] + ex[3]
    te = jnp.zeros((t, LANES), I32)
    gates = jnp.zeros((t, LANES), F32)
    for k in range(TOP_K_EXPERTS):
        te = jnp.where(lane == k, top_e[k], te)
        gates = jnp.where(lane == k, ex[k] / denom, gates)
    te_ref[...] = te
    gates_ref[...] = gates


def _mix_and_route(x, ya, mc, gb, ga1, sc2, sh2, gpost, gpre, wao, wmo, wr, br, *, rows):
    b, t, d = x.shape
    nblk = t // rows
    per_row = ga1.shape[1] != 1

    def row_spec(width, rowwise=True):
        if rowwise:
            return pl.BlockSpec((1, rows, width), lambda i, j: (i, j, 0))
        return pl.BlockSpec((1, 1, width), lambda i, j: (i, 0, 0))

    flat = lambda w: pl.BlockSpec((rows, w), lambda i, j: (i * nblk + j, 0))
    n = b * t
    return pl.pallas_call(
        _mix_kernel,
        grid=(b, nblk),
        in_specs=[row_spec(d), row_spec(D_ATTN), row_spec(d), row_spec(d),
                  row_spec(d, per_row), row_spec(d, per_row), row_spec(d, per_row),
                  _const_spec(gpost.shape), _const_spec(gpre.shape), _const_spec(wao.shape),
                  _const_spec(wmo.shape), _const_spec(wr.shape), _const_spec(br.shape)],
        out_specs=[row_spec(d),
                   pl.BlockSpec((rows, d // LANES, LANES), lambda i, j: (i * nblk + j, 0, 0)),
                   flat(LANES), flat(LANES)],
        out_shape=[jax.ShapeDtypeStruct((b, t, d), F32),
                   jax.ShapeDtypeStruct((n, d // LANES, LANES), F32),
                   jax.ShapeDtypeStruct((n, LANES), I32),
                   jax.ShapeDtypeStruct((n, LANES), F32)],
        compiler_params=_params("arbitrary", "arbitrary"),
        name="mix_and_route",
    )(x, ya, mc, gb, ga1, sc2, sh2, gpost, gpre, wao, wmo, wr, br)


def _rank_kernel(te_ref, rank_ref, cnt_ref, carry_ref):
    @pl.when(pl.program_id(0) == 0)
    def _():
        carry_ref[...] = jnp.zeros(carry_ref.shape, F32)

    te = te_ref[...]
    t = te.shape[0]
    lane = lax.broadcasted_iota(I32, (t, LANES), 1)
    hits = [lane == te[:, k:k + 1] for k in range(TOP_K_EXPERTS)]
    onehot = jnp.zeros((t, LANES), F32)
    for hit in hits:
        onehot = onehot + jnp.where(hit, 1.0, 0.0)
    lower = jnp.where(lax.broadcasted_iota(I32, (t, t), 0) > lax.broadcasted_iota(I32, (t, t), 1), 1.0, 0.0)
    before = jnp.dot(lower.astype(BF16), onehot.astype(BF16), preferred_element_type=F32) + carry_ref[...]
    rank = jnp.zeros((t, LANES), F32)
    for k, hit in enumerate(hits):
        rk = jnp.sum(jnp.where(hit, before, 0.0), axis=1, keepdims=True)
        rank = jnp.where(lane == k, rk, rank)
    rank_ref[...] = rank.astype(I32)
    carry_ref[...] = carry_ref[...] + jnp.sum(onehot, axis=0, keepdims=True)
    cnt_ref[...] = jnp.broadcast_to(carry_ref[...], cnt_ref.shape)


def _expert_ranks(te, *, rows):
    n = te.shape[0]
    return pl.pallas_call(
        _rank_kernel,
        grid=(n // rows,),
        in_specs=[pl.BlockSpec((rows, LANES), lambda i: (i, 0))],
        out_specs=[pl.BlockSpec((rows, LANES), lambda i: (i, 0)),
                   pl.BlockSpec((SUBLANES, LANES), lambda i: (0, 0))],
        out_shape=[jax.ShapeDtypeStruct((n, LANES), I32), jax.ShapeDtypeStruct((SUBLANES, LANES), F32)],
        scratch_shapes=[pltpu.VMEM((1, LANES), F32)],
        compiler_params=_params("arbitrary"),
        name="expert_ranks",
    )(te)


def _scatter_kernel(dest_ref, h2_ref, init_ref, xs_ref, sem, *, rows):
    del init_ref
    base = pl.program_id(0) * rows * TOP_K_EXPERTS

    def row_copy(n, k):
        return pltpu.make_async_copy(h2_ref.at[n], xs_ref.at[dest_ref[base + n * TOP_K_EXPERTS + k]], sem)

    def start(n, carry):
        for k in range(TOP_K_EXPERTS):
            row_copy(n, k).start()
        return carry

    def wait(n, carry):
        for k in range(TOP_K_EXPERTS):
            row_copy(n, k).wait()
        return carry

    lax.fori_loop(0, rows, start, 0)
    lax.fori_loop(0, rows, wait, 0)


def _scatter_rows(dest_flat, h2, cap, *, rows):
    n, s, _ = h2.shape
    init = jnp.zeros((cap, s, LANES), F32)
    grid_spec = pltpu.PrefetchScalarGridSpec(
        num_scalar_prefetch=1,
        grid=(n // rows,),
        in_specs=[pl.BlockSpec((rows, s, LANES), lambda i, d: (i, 0, 0)),
                  pl.BlockSpec(memory_space=pl.ANY)],
        out_specs=pl.BlockSpec(memory_space=pl.ANY),
        scratch_shapes=[pltpu.SemaphoreType.DMA(())],
    )
    return pl.pallas_call(
        functools.partial(_scatter_kernel, rows=rows),
        grid_spec=grid_spec,
        out_shape=jax.ShapeDtypeStruct((cap, s, LANES), F32),
        input_output_aliases={2: 0},
        compiler_params=_params("arbitrary"),
        name="scatter_rows",
    )(dest_flat, h2, init)


def _expert_kernel(be_ref, nu_ref, xs_ref, wgu_ref, bgu_ref, wdn_ref, bdn_ref, yb_ref):
    del be_ref

    @pl.when(pl.program_id(0) < nu_ref[0])
    def _():
        nseg = xs_ref.shape[1]
        x = jnp.concatenate([xs_ref[:, j, :] for j in range(nseg)], axis=1).astype(BF16)
        h = jnp.dot(x, wgu_ref[0], preferred_element_type=F32) + bgu_ref[0]
        gate = jnp.minimum(h[:, :D_FF], SWIGLU_LIMIT)
        up = jnp.clip(h[:, D_FF:], -SWIGLU_LIMIT, SWIGLU_LIMIT)
        act = (up + 1.0) * gate * jax.nn.sigmoid(SWIGLU_ALPHA * gate)
        y = jnp.dot(act.astype(BF16), wdn_ref[0], preferred_element_type=F32) + bdn_ref[0]
        for j in range(nseg):
            yb_ref[:, j, :] = y[:, j * LANES:(j + 1) * LANES]

    @pl.when(pl.program_id(0) >= nu_ref[0])
    def _():
        yb_ref[...] = jnp.zeros(yb_ref.shape, yb_ref.dtype)


def _expert_ffn(block_e, n_used, xs, w_gu, b_gu, w_dn, b_dn):
    cap, s, _ = xs.shape
    d = s * LANES
    nblk = cap // MOE_ROWS
    rows_map = lambda i, be, nu: (jnp.minimum(i, nu[0] - 1), 0, 0)
    by_expert = lambda i, be, nu: (be[i], 0, 0)
    grid_spec = pltpu.PrefetchScalarGridSpec(
        num_scalar_prefetch=2,
        grid=(nblk,),
        in_specs=[pl.BlockSpec((MOE_ROWS, s, LANES), rows_map),
                  pl.BlockSpec((1, d, 2 * D_FF), by_expert),
                  pl.BlockSpec((1, 1, 2 * D_FF), by_expert),
                  pl.BlockSpec((1, D_FF, d), by_expert),
                  pl.BlockSpec((1, 1, d), by_expert)],
        out_specs=pl.BlockSpec((MOE_ROWS, s, LANES), lambda i, be, nu: (i, 0, 0)),
    )
    return pl.pallas_call(
        _expert_kernel,
        grid_spec=grid_spec,
        out_shape=jax.ShapeDtypeStruct((cap, s, LANES), F32),
        compiler_params=_params("arbitrary"),
        name="expert_ffn",
    )(block_e, n_used, xs, w_gu, b_gu, w_dn, b_dn)


def _combine_kernel(dest_ref, gates_ref, x1_ref, ga2_ref, g_ref, yb_ref, o_ref, buf, sem, *, rows):
    nblk = pl.num_programs(1)
    step = pl.program_id(0) * nblk + pl.program_id(1)
    n_steps = pl.num_programs(0) * nblk

    def row_copy(st, slot, n, k):
        src = dest_ref[(st * rows + n) * TOP_K_EXPERTS + k]
        return pltpu.make_async_copy(yb_ref.at[src], buf.at[slot, k, n], sem.at[slot])

    def fetch(st, slot):
        def start(n, carry):
            for k in range(TOP_K_EXPERTS):
                row_copy(st, slot, n, k).start()
            return carry

        lax.fori_loop(0, rows, start, 0)

    @pl.when(step == 0)
    def _():
        fetch(0, 0)

    @pl.when(step + 1 < n_steps)
    def _():
        fetch(step + 1, (step + 1) % 2)

    slot = step % 2

    def wait(n, carry):
        for k in range(TOP_K_EXPERTS):
            row_copy(step, slot, n, k).wait()
        return carry

    lax.fori_loop(0, rows, wait, 0)
    gates = gates_ref[...]
    nseg = buf.shape[3]
    f = None
    for k in range(TOP_K_EXPERTS):
        yk = jnp.concatenate([buf[slot, k, :, j, :] for j in range(nseg)], axis=1)
        term = yk * gates[:, k:k + 1]
        f = term if f is None else f + term
    o_ref[0] = x1_ref[0] + ga2_ref[0] * _rms(f, g_ref[...])


def _combine(dest_flat, gates, x1, ga2, g, yb, *, rows):
    b, t, d = x1.shape
    nblk = t // rows
    per_row = ga2.shape[1] != 1
    row_blk = pl.BlockSpec((1, rows, d), lambda i, j, ds: (i, j, 0))
    mod_blk = row_blk if per_row else pl.BlockSpec((1, 1, d), lambda i, j, ds: (i, 0, 0))
    grid_spec = pltpu.PrefetchScalarGridSpec(
        num_scalar_prefetch=1,
        grid=(b, nblk),
        in_specs=[pl.BlockSpec((rows, LANES), lambda i, j, ds: (i * nblk + j, 0)),
                  row_blk, mod_blk,
                  pl.BlockSpec(g.shape, lambda i, j, ds: (0, 0)),
                  pl.BlockSpec(memory_space=pl.ANY)],
        out_specs=row_blk,
        scratch_shapes=[pltpu.VMEM((2, TOP_K_EXPERTS, rows, d // LANES, LANES), F32),
                        pltpu.SemaphoreType.DMA((2,))],
    )
    return pl.pallas_call(
        functools.partial(_combine_kernel, rows=rows),
        grid_spec=grid_spec,
        out_shape=jax.ShapeDtypeStruct((b, t, d), F32),
        compiler_params=_params("arbitrary", "arbitrary"),
        name="combine",
    )(dest_flat, gates, x1, ga2, g, yb)


def _moe(x1, h2, te, gates, ga2, g_post, w_gu, b_gu, w_dn, b_dn, *, rank_rows, io_rows):
    n = h2.shape[0]
    n_slots = n * TOP_K_EXPERTS
    rank, counts = _expert_ranks(te, rows=rank_rows)
    counts = counts[0, :N_EXPERTS].astype(I32)
    padded = (counts + MOE_ROWS - 1) // MOE_ROWS * MOE_ROWS
    pend = jnp.cumsum(padded)
    pstart = pend - padded
    nblk = (n_slots + N_EXPERTS * (MOE_ROWS - 1) + MOE_ROWS - 1) // MOE_ROWS
    n_used = (pend[-1] // MOE_ROWS).astype(I32)
    blk_start = jnp.arange(nblk, dtype=I32) * MOE_ROWS
    block_e = jnp.minimum(jnp.sum(pend[None, :] <= blk_start[:, None], axis=1), N_EXPERTS - 1).astype(I32)
    block_e = jnp.where(jnp.arange(nblk) < n_used, block_e, block_e[jnp.maximum(n_used - 1, 0)])
    top_e = te[:, :TOP_K_EXPERTS]
    dest = (pstart[top_e] + rank[:, :TOP_K_EXPERTS]).reshape(-1).astype(I32)
    xs = _scatter_rows(dest, h2, nblk * MOE_ROWS, rows=io_rows)
    yb = _expert_ffn(block_e, n_used.reshape(1), xs, w_gu, b_gu, w_dn, b_dn)
    return _combine(dest, gates, x1, ga2, g_post, yb, rows=io_rows)


def _t5_bucket(dist):
    n = jnp.maximum(dist, 0)
    max_exact = N_BUCKETS // 2
    nf = jnp.maximum(n, 1).astype(F32)
    large = max_exact + (jnp.log(nf / max_exact) / math.log(MAX_DISTANCE / max_exact)
                         * (N_BUCKETS - max_exact)).astype(I32)
    large = jnp.minimum(large, N_BUCKETS - 1)
    return jnp.where(n < max_exact, n, large)


def _split_in_proj(w_in):
    d = w_in.shape[0]
    wa = w_in[:, :OFF_KI].astype(BF16)
    wk = jnp.pad(w_in[:, OFF_KI:OFF_GA], ((0, 0), (0, LANES - (OFF_GA - OFF_KI)))).astype(BF16)
    wga = w_in[:, OFF_GA:OFF_GA + d].astype(BF16)
    wgb = w_in[:, OFF_GA + d:OFF_GA + 2 * d].astype(BF16)
    return wa, wk, wga, wgb


def kernel(x_prompt, x_sample, cache_k, cache_v, cache_kidx, state_conv, page_table, c_prompt, c_sample, rel_bias, w_mod, b_mod, g_pre_mix, g_post_mix, w_in, conv_w, w_conv_out, w_attn_out, w_mix_out, g_pre_ffn, g_post_ffn, w_router, b_router, w_gu, b_gu, w_dn, b_dn):
    depth = w_mod.shape[0]
    bp, seq, d = x_prompt.shape
    nb, dec_seq, _ = x_sample.shape
    assert dec_seq == 1
    page = cache_k.shape[2]
    past = page_table.shape[1] * page
    tq, tk = 512, 512
    rows_p = 256
    s_topk = min(TOPK_MAX, (past + dec_seq) // 4)

    tab = rel_bias.astype(F32)[_t5_bucket(jnp.arange(2 * LANES + 1, dtype=I32))].T
    rel = tab - tab[:, MAX_DISTANCE:MAX_DISTANCE + 1]
    ii = jnp.arange(LANES, dtype=I32)[:, None]
    jj = jnp.arange(LANES, dtype=I32)[None, :]
    bias_tiles = jnp.stack([rel[:, jnp.clip(ii - jj, 0, 2 * LANES)],
                            rel[:, jnp.clip(LANES + ii - jj, 0, 2 * LANES)]], axis=1)
    bias_s = tab[:, jnp.clip(past - jnp.arange(past + LANES, dtype=I32), 0, MAX_DISTANCE)]
    cache_k_t = jnp.transpose(cache_k, (0, 1, 3, 4, 2))
    cache_v_t = jnp.transpose(cache_v, (0, 1, 3, 4, 2))
    cache_kidx_t = jnp.transpose(cache_kidx, (0, 1, 3, 2))

    xp = x_prompt
    xs_rows = x_sample.reshape(1, nb, d)
    c_all = jnp.concatenate([c_prompt, c_sample], axis=0)
    n_c = c_all.shape[0]
    c_all = jnp.pad(c_all, ((0, -n_c % SUBLANES), (0, 0)))
    outs = [[] for _ in range(8)]
    for l in range(depth):
        mod = _modulation(c_all, w_mod[l].astype(BF16), b_mod[l])
        mod_p = [m[:, None, :] for m in jnp.split(mod[:bp], 6, axis=-1)]
        mod_s = [m[None] for m in jnp.split(mod[bp:bp + nb], 6, axis=-1)]
        wa, wk, wga, wgb = _split_in_proj(w_in[l])
        wts = (wa, wk, wga, wgb, conv_w[l], w_conv_out[l].astype(BF16))
        g1 = g_pre_mix[l].reshape(1, d)
        wao = w_attn_out[l].astype(BF16)
        wmo = w_mix_out[l].astype(BF16)
        wr = jnp.pad(w_router[l], ((0, 0), (0, LANES - N_EXPERTS))).astype(BF16)
        br = jnp.pad(b_router[l], (0, LANES - N_EXPERTS)).reshape(1, LANES)
        gpost = g_post_mix[l].reshape(1, d)
        gpre2 = g_pre_ffn[l].reshape(1, d)
        gpost2 = g_post_ffn[l].reshape(1, d)
        wgu = w_gu[l].astype(BF16)
        bgu = b_gu[l].reshape(N_EXPERTS, 1, 2 * D_FF)
        wdn = w_dn[l].astype(BF16)
        bdn = b_dn[l].reshape(N_EXPERTS, 1, d)

        st = state_conv[l].reshape(1, nb, 2 * D_CONV)
        q, k, v, qi, kiwi, mc, gb, u = _in_projection(
            xs_rows, mod_s[1], mod_s[0], st, g1, wts, seq_conv=False, rows=nb)
        kidx = kiwi[0, :, :IDX_DIM]
        wi = kiwi[0, :, IDX_DIM:IDX_DIM + N_IDX_HEADS]
        scores = _sample_scores(page_table, qi.reshape(nb, N_IDX_HEADS, IDX_DIM), wi[:, :, None],
                                kidx[:, None, :], cache_kidx_t, l)
        madd = _sample_select(scores.reshape(nb, past + LANES), s_topk)
        k3 = k.reshape(nb, N_HEADS, HEAD_DIM)
        v3 = v.reshape(nb, N_HEADS, HEAD_DIM)
        ya = _sample_attention(page_table, q.astype(F32).reshape(nb, N_HEADS, HEAD_DIM), k3, v3,
                               madd.reshape(nb, 1, past + LANES), bias_s, cache_k_t, cache_v_t, l,
                               pages_per_step=16)
        ya = jnp.swapaxes(ya[:, :, :N_HEADS], 1, 2)
        x1, h2, te, gates = _mix_and_route(xs_rows, ya.reshape(1, nb, D_ATTN), mc, gb, mod_s[2], mod_s[4],
                                           mod_s[3], gpost, gpre2, wao, wmo, wr, br, rows=nb)
        xs_rows = _moe(x1, h2, te, gates, mod_s[5], gpost2, wgu, bgu, wdn, bdn, rank_rows=nb, io_rows=nb)
        outs[4].append(k.reshape(nb, dec_seq, N_HEADS, HEAD_DIM))
        outs[5].append(v.reshape(nb, dec_seq, N_HEADS, HEAD_DIM))
        outs[6].append(kidx[:, None, :])
        outs[7].append(jnp.stack([state_conv[l][:, 1, :], u[0]], axis=1))

        prefix = jnp.zeros((bp, 1, 2 * D_CONV), F32)
        q, k, v, qi, kiwi, mc, gb, u_tail = _in_projection(
            xp, mod_p[1], mod_p[0], prefix, g1, wts, seq_conv=True, rows=rows_p)
        kidx = kiwi[..., :IDX_DIM]
        by_head = lambda a: jnp.swapaxes(a.reshape(bp, seq, N_HEADS, HEAD_DIM), 1, 2)
        ya = _prompt_attention(by_head(q), by_head(qi), kiwi, jnp.swapaxes(kidx, 1, 2).astype(BF16),
                               (jnp.swapaxes(k, 1, 2) * LOG2E).astype(BF16), v.astype(BF16),
                               bias_tiles * LOG2E, tq=tq, tk=tk)
        x1, h2, te, gates = _mix_and_route(xp, ya, mc, gb, mod_p[2], mod_p[4], mod_p[3], gpost, gpre2,
                                           wao, wmo, wr, br, rows=rows_p)
        xp = _moe(x1, h2, te, gates, mod_p[5], gpost2, wgu, bgu, wdn, bdn, rank_rows=512, io_rows=rows_p)
        outs[0].append(k.reshape(bp, seq, N_HEADS, HEAD_DIM))
        outs[1].append(v.reshape(bp, seq, N_HEADS, HEAD_DIM))
        outs[2].append(kidx)
        outs[3].append(u_tail[:, SUBLANES - (CONV_W - 1):, :])
    return (xp, xs_rows.reshape(nb, dec_seq, d)) + tuple(jnp.stack(o) for o in outs)
```

```python
import functools
import math

import jax
import jax.numpy as jnp
from jax import lax
from jax.experimental import pallas as pl
from jax.experimental.pallas import tpu as pltpu

F32 = jnp.float32
BF16 = jnp.bfloat16
I32 = jnp.int32

D_CONV = 512
CONV_W = 3
N_HEADS = 8
HEAD_DIM = 64
D_ATTN = N_HEADS * HEAD_DIM
ATTN_SCALE = HEAD_DIM ** -0.5
N_IDX_HEADS = 8
IDX_DIM = 64
INDEX_SCALE = (N_IDX_HEADS * IDX_DIM) ** -0.5
TOPK_MAX = 256
N_BUCKETS = 32
MAX_DISTANCE = 128
N_EXPERTS = 32
TOP_K_EXPERTS = 4
D_FF = 1024
SWIGLU_LIMIT = 7.0
SWIGLU_ALPHA = 1.702
EPS = 1e-6

LANES = 128
SUBLANES = 8
VMEM_LIMIT_BYTES = 56 * 1024 * 1024

OFF_Q = 3 * D_CONV
OFF_K = OFF_Q + D_ATTN
OFF_V = OFF_K + D_ATTN
OFF_QI = OFF_V + D_ATTN
OFF_KI = OFF_QI + N_IDX_HEADS * IDX_DIM
OFF_WI = OFF_KI + IDX_DIM
OFF_GA = OFF_WI + N_IDX_HEADS

LOG2E = 1.4426950408889634
INT_MIN = -2 ** 31
NEG_BIG = -1e30
MOE_ROWS = 512


def _params(*sem):
    return pltpu.CompilerParams(dimension_semantics=sem, vmem_limit_bytes=VMEM_LIMIT_BYTES)


def _const_spec(shape):
    zeros = (0,) * len(shape)
    return pl.BlockSpec(shape, lambda *_: zeros, pipeline_mode=pl.Buffered(1))


def _rms(x, g):
    ms = jnp.mean(x * x, axis=-1, keepdims=True)
    return (x * lax.rsqrt(ms + EPS)) * g


def _sort_key(x):
    bits = pltpu.bitcast(x, I32)
    return jnp.where(bits < 0, bits ^ jnp.int32(0x7FFFFFFF), bits)


def _mod_kernel(c_ref, w_ref, b_ref, o_ref):
    c = c_ref[...]
    s = c * jax.nn.sigmoid(c)
    o_ref[...] = jnp.dot(s.astype(BF16), w_ref[...], preferred_element_type=F32) + b_ref[...]


def _modulation(c, w_mod, b_mod):
    n, d = c.shape
    n_out = w_mod.shape[1]
    bn = 1024
    return pl.pallas_call(
        _mod_kernel,
        grid=(n_out // bn,),
        in_specs=[pl.BlockSpec((n, d), lambda j: (0, 0)),
                  pl.BlockSpec((d, bn), lambda j: (0, j)),
                  pl.BlockSpec((1, bn), lambda j: (0, j))],
        out_specs=pl.BlockSpec((n, bn), lambda j: (0, j)),
        out_shape=jax.ShapeDtypeStruct((n, n_out), F32),
        compiler_params=_params("arbitrary"),
        name="modulation",
    )(c, w_mod, b_mod.reshape(1, n_out))


def _inproj_kernel(x_ref, sc_ref, sh_ref, st_ref, g_ref, wa_ref, wk_ref, wga_ref, wgb_ref, cw_ref, wco_ref,
                   q_ref, k_ref, v_ref, qi_ref, kiwi_ref, mc_ref, gb_ref, u_ref, carry_ref, *, seq_conv):
    x = x_ref[0]
    t = x.shape[0]
    h = _rms(x, g_ref[...]) * (1.0 + sc_ref[0]) + sh_ref[0]
    hb = h.astype(BF16)

    def proj(lo, hi):
        return jnp.dot(hb, wa_ref[:, lo:hi], preferred_element_type=F32)

    b_gate = proj(0, D_CONV)
    u = proj(D_CONV, 2 * D_CONV) * proj(2 * D_CONV, 3 * D_CONV)
    if seq_conv:
        @pl.when(pl.program_id(1) == 0)
        def _():
            carry_ref[0:1, :] = st_ref[0][:, 0:D_CONV]
            carry_ref[1:2, :] = st_ref[0][:, D_CONV:2 * D_CONV]

        cm2 = carry_ref[0:1, :]
        cm1 = carry_ref[1:2, :]
        rows = lax.broadcasted_iota(I32, u.shape, 0)
        prev1 = jnp.where(rows == 0, cm1, pltpu.roll(u, 1, 0))
        prev2 = jnp.where(rows == 0, cm2, jnp.where(rows == 1, cm1, pltpu.roll(u, 2, 0)))
        carry_ref[...] = u[t - 2:t, :]
        u_ref[0] = u[t - SUBLANES:t, :]
    else:
        prev2 = st_ref[0][:, 0:D_CONV]
        prev1 = st_ref[0][:, D_CONV:2 * D_CONV]
        u_ref[0] = u
    cw = cw_ref[...]
    y_conv = b_gate * (cw[0:1, :] * prev2 + cw[1:2, :] * prev1 + cw[2:3, :] * u)
    g_a = jax.nn.sigmoid(jnp.dot(hb, wga_ref[...], preferred_element_type=F32))
    mc_ref[0] = g_a * jnp.dot(y_conv.astype(BF16), wco_ref[...], preferred_element_type=F32)
    gb_ref[0] = jax.nn.sigmoid(jnp.dot(hb, wgb_ref[...], preferred_element_type=F32))
    q_ref[0] = (proj(OFF_Q, OFF_K) * ATTN_SCALE).astype(BF16)
    k_ref[0] = proj(OFF_K, OFF_V)
    v_ref[0] = proj(OFF_V, OFF_QI)
    qi_ref[0] = proj(OFF_QI, OFF_KI).astype(BF16)
    kiwi_ref[0] = jnp.dot(hb, wk_ref[...], preferred_element_type=F32)


def _in_projection(x, sc, sh, st, g, wts, *, seq_conv, rows):
    b, t, d = x.shape
    nblk = t // rows
    tm = 1 if sc.shape[1] == 1 else rows
    ts = 1 if seq_conv else rows
    tu = SUBLANES if seq_conv else rows
    wa, wk, wga, wgb, cw, wco = wts

    def row_spec(width, per_row):
        if per_row:
            return pl.BlockSpec((1, rows, width), lambda i, j: (i, j, 0))
        return pl.BlockSpec((1, 1, width), lambda i, j: (i, 0, 0))

    out_widths = (D_ATTN, D_ATTN, D_ATTN, N_IDX_HEADS * IDX_DIM, LANES, d, d)
    out_dtypes = (BF16, F32, F32, BF16, F32, F32, F32)
    out_shape = [jax.ShapeDtypeStruct((b, t, w), dt) for w, dt in zip(out_widths, out_dtypes)]
    out_specs = [row_spec(w, True) for w in out_widths]
    if seq_conv:
        out_shape.append(jax.ShapeDtypeStruct((b, tu, D_CONV), F32))
        out_specs.append(pl.BlockSpec((1, tu, D_CONV), lambda i, j: (i, 0, 0)))
    else:
        out_shape.append(jax.ShapeDtypeStruct((b, t, D_CONV), F32))
        out_specs.append(row_spec(D_CONV, True))
    return pl.pallas_call(
        functools.partial(_inproj_kernel, seq_conv=seq_conv),
        grid=(b, nblk),
        in_specs=[row_spec(d, True), row_spec(d, tm != 1), row_spec(d, tm != 1),
                  row_spec(2 * D_CONV, ts != 1),
                  _const_spec(g.shape), _const_spec(wa.shape), _const_spec(wk.shape),
                  _const_spec(wga.shape), _const_spec(wgb.shape), _const_spec(cw.shape),
                  _const_spec(wco.shape)],
        out_specs=out_specs,
        out_shape=out_shape,
        scratch_shapes=[pltpu.VMEM((2, D_CONV), F32)],
        compiler_params=_params("arbitrary", "arbitrary"),
        name="in_projection",
    )(x, sc, sh, st, g, wa, wk, wga, wgb, cw, wco)


ROW_TILE = 128
ATT_ROWS = 256


def _lane_rep(col, width=LANES):
    return jnp.broadcast_to(col, (col.shape[0], width))


def _tile_lanes(x, n):
    return x if n == 1 else jnp.concatenate([x] * n, axis=1)


def _count_rows(keys_ref, nblk, tq, tk, pred_of_tile):
    out = []
    for r in range(tq // ROW_TILE):
        rows = slice(r * ROW_TILE, (r + 1) * ROW_TILE)
        pred = pred_of_tile(r)

        def body(kb, acc, rows=rows, pred=pred):
            off = pl.multiple_of(kb * tk, tk)
            for c in range(tk // LANES):
                blk = keys_ref[rows, pl.ds(off + c * LANES, LANES)]
                acc = acc + jnp.where(pred(blk), 1.0, 0.0)
            return acc

        out.append(lax.fori_loop(0, nblk, body, jnp.zeros((ROW_TILE, LANES), F32)))
    assert keys_ref.shape[1] // LANES < 256
    partial = jnp.concatenate(out, axis=0).astype(BF16)
    return jnp.dot(partial, jnp.ones((LANES, LANES), BF16), preferred_element_type=F32).astype(I32)


def _kth_largest_key(keys_ref, nblk, tq, tk, k):
    def bit_body(i, kk):
        cand = kk + (jnp.int32(1) << (31 - i))

        def pred_of_tile(r):
            cb = cand[r * ROW_TILE:(r + 1) * ROW_TILE]
            return lambda blk: blk >= cb

        cnt = _count_rows(keys_ref, nblk, tq, tk, pred_of_tile)
        return jnp.where(cnt >= k, cand, kk)

    return lax.fori_loop(0, 32, bit_body, jnp.full((tq, LANES), INT_MIN, I32))


def _pattn_kernel(q_ref, qi_ref, kiwi_ref, kidxt_ref, kt_ref, v_ref, bias_ref, upper_ref, o_ref,
                  keys_ref, wb_ref, madd_ref, m_ref, acc_ref, eqc_ref, *, tq, tk, topk):
    qb = pl.program_id(1)
    kd = (qb * tq) // tk
    nkb = kd + 1
    reps = tk // LANES
    wi = kiwi_ref[0][:, IDX_DIM:IDX_DIM + N_IDX_HEADS]
    for h in range(N_IDX_HEADS):
        wb_ref[h] = _lane_rep(wi[:, h:h + 1])

    col_w = 2 * LANES
    row_i = lax.broadcasted_iota(I32, (ROW_TILE, col_w), 0)
    col_i = lax.broadcasted_iota(I32, (ROW_TILE, col_w), 1)

    def score_block(kb, carry):
        off = pl.multiple_of(kb * tk, tk)
        for r in range(tq // ROW_TILE):
            rows = slice(r * ROW_TILE, (r + 1) * ROW_TILE)
            for c in range(tk // col_w):
                kx = kidxt_ref[0, :, pl.ds(off + c * col_w, col_w)]
                acc = jnp.zeros((ROW_TILE, col_w), F32)
                for h in range(N_IDX_HEADS):
                    s = jnp.dot(qi_ref[0, h, rows, :], kx, preferred_element_type=F32)
                    acc = acc + jnp.maximum(s, 0.0) * _tile_lanes(wb_ref[h, rows, :], col_w // LANES)
                key = _sort_key(acc * INDEX_SCALE)
                valid = (off + c * col_w + col_i) <= (qb * tq + r * ROW_TILE + row_i)
                keys_ref[rows, pl.ds(off + c * col_w, col_w)] = jnp.where(valid, key, INT_MIN)
        return carry

    lax.fori_loop(0, nkb, score_block, 0)

    kth = _kth_largest_key(keys_ref, nkb, tq, tk, topk)

    def gt_of_tile(r):
        kb_ = kth[r * ROW_TILE:(r + 1) * ROW_TILE]
        return lambda blk: blk > kb_

    cnt_gt = _count_rows(keys_ref, nkb, tq, tk, gt_of_tile)
    need = jnp.where(kth == INT_MIN, 0, topk - cnt_gt).astype(F32)
    kth_w = _tile_lanes(kth, reps)
    need_w = _tile_lanes(need, reps)

    m_ref[...] = jnp.full(m_ref.shape, NEG_BIG, F32)
    acc_ref[...] = jnp.zeros(acc_ref.shape, F32)
    eqc_ref[...] = jnp.zeros(eqc_ref.shape, F32)

    def near_bias(h, kb, r):
        nrow = min(ATT_ROWS, LANES)
        strips = []
        for r0 in range(r * ATT_ROWS, (r + 1) * ATT_ROWS, nrow):
            i, sub = divmod(r0, LANES)
            t0 = bias_ref[h, 0, sub:sub + nrow, :]
            t1 = bias_ref[h, 1, sub:sub + nrow, :]
            tiles = []
            for j in range(reps):
                d = qb * (tq // LANES) + i - kb * reps - j
                tiles.append(jnp.where(d == 0, t0, jnp.where(d == 1, t1, 0.0)))
            strips.append(jnp.concatenate(tiles, axis=1))
        return strips[0] if len(strips) == 1 else jnp.concatenate(strips, axis=0)

    lane_k = lax.broadcasted_iota(I32, (tk, LANES), 1)
    own = [jnp.where((lane_k < HEAD_DIM) == (par == 0), 1.0, 0.0).astype(BF16) for par in (0, 1)]
    other = [own[1], own[0]]

    def attend_rows(off, rows, bias_of_head):
        madd = madd_ref[rows, :]
        for h in range(N_HEADS):
            s = jnp.dot(q_ref[0, h, rows, :], kt_ref[0, h * HEAD_DIM:(h + 1) * HEAD_DIM, pl.ds(off, tk)],
                        preferred_element_type=F32) + madd
            if bias_of_head is not None:
                s = s + bias_of_head(h)
            m_old = m_ref[h, rows, :]
            m_new = jnp.maximum(m_old, _lane_rep(jnp.max(s, axis=1, keepdims=True)))
            alpha = jnp.exp2(m_old - m_new)
            p = jnp.exp2(s - _tile_lanes(m_new, reps))
            pair = slice((h // 2) * LANES, (h // 2 + 1) * LANES)
            v_aug = v_ref[0, pl.ds(off, tk), pair] * own[h % 2] + other[h % 2]
            pv = jnp.dot(p.astype(BF16), v_aug, preferred_element_type=F32)
            acc_ref[h, rows, :] = alpha * acc_ref[h, rows, :] + pv
            m_ref[h, rows, :] = m_new

    def attend_block(kb, near):
        off = pl.multiple_of(kb * tk, tk)
        sk = keys_ref[:, pl.ds(off, tk)]
        eq = sk == kth_w
        eqf = jnp.where(eq, 1.0, 0.0)
        rank = jnp.dot(eqf.astype(BF16), upper_ref[...], preferred_element_type=F32)
        rank = rank + _tile_lanes(eqc_ref[...], reps)
        eqc_ref[...] = eqc_ref[...] + _lane_rep(jnp.sum(eqf, axis=1, keepdims=True))
        madd_ref[...] = jnp.where(sk > kth_w, 0.0, jnp.where(eq, jnp.where(rank < need_w, 0.0, NEG_BIG), NEG_BIG))
        for r in range(tq // ATT_ROWS):
            bias_of_head = (lambda h, r=r: near_bias(h, kb, r)) if near else None
            attend_rows(off, slice(r * ATT_ROWS, (r + 1) * ATT_ROWS), bias_of_head)

    def far_block(kb, carry):
        attend_block(kb, False)
        return carry

    lax.fori_loop(0, jnp.maximum(kd - 1, 0), far_block, 0)

    @pl.when(kd >= 1)
    def _():
        attend_block(kd - 1, True)

    attend_block(kd, True)
    lane_q = lax.broadcasted_iota(I32, (tq, LANES), 1)
    for j in range(N_HEADS // 2):
        even = acc_ref[2 * j]
        odd = acc_ref[2 * j + 1]
        out = jnp.where(lane_q < HEAD_DIM, even / pltpu.roll(even, HEAD_DIM, 1), odd / pltpu.roll(odd, HEAD_DIM, 1))
        o_ref[0, :, j * LANES:(j + 1) * LANES] = out.astype(o_ref.dtype)


def _prompt_attention(q, qi, kiwi, kidxt, kt, v, bias_tiles, *, tq, tk):
    b, _, s, _ = q.shape
    topk = min(TOPK_MAX, s // 4)
    assert tk % (2 * LANES) == 0 and tk % tq == 0 and tq % ROW_TILE == 0 and s % tk == 0
    assert LANES >= MAX_DISTANCE and tk >= 2 * LANES
    upper = jnp.triu(jnp.ones((tk, tk), BF16), 1)
    heads = lambda: pl.BlockSpec((1, N_HEADS, tq, HEAD_DIM), lambda i, j: (i, 0, j, 0))
    per_b = lambda r, c: pl.BlockSpec((1, r, c), lambda i, j: (i, 0, 0), pipeline_mode=pl.Buffered(1))
    stat = lambda w: pltpu.VMEM((N_HEADS, tq, w), F32)
    return pl.pallas_call(
        functools.partial(_pattn_kernel, tq=tq, tk=tk, topk=topk),
        grid=(b, s // tq),
        in_specs=[heads(), heads(), pl.BlockSpec((1, tq, LANES), lambda i, j: (i, j, 0)),
                  per_b(IDX_DIM, s), per_b(D_ATTN, s), per_b(s, D_ATTN),
                  _const_spec(bias_tiles.shape), _const_spec(upper.shape)],
        out_specs=pl.BlockSpec((1, tq, D_ATTN), lambda i, j: (i, j, 0)),
        out_shape=jax.ShapeDtypeStruct((b, s, D_ATTN), BF16),
        scratch_shapes=[pltpu.VMEM((tq, s), I32), stat(LANES), pltpu.VMEM((tq, tk), F32),
                        stat(LANES), stat(LANES), pltpu.VMEM((tq, LANES), F32)],
        compiler_params=_params("arbitrary", "arbitrary"),
        name="prompt_attention",
    )(q, qi, kiwi, kidxt, kt, v, bias_tiles, upper)


def _sscore_kernel(pt_ref, qi_ref, wi_ref, kn_ref, cache_ref, o_ref, buf, sem, *, layer, n_pages, page):
    b = pl.program_id(0)
    nb = pl.num_programs(0)
    past = n_pages * page

    def page_copy(bb, slot, p):
        return pltpu.make_async_copy(cache_ref.at[layer, pt_ref[bb, p]], buf.at[slot, p], sem.at[slot])

    def fetch(bb, slot):
        for p in range(n_pages):
            page_copy(bb, slot, p).start()

    @pl.when(b == 0)
    def _():
        fetch(0, 0)

    @pl.when(b + 1 < nb)
    def _():
        fetch(b + 1, (b + 1) % 2)

    slot = b % 2
    for p in range(n_pages):
        page_copy(b, slot, p).wait()
    qi = qi_ref[0]
    wi = wi_ref[0]
    group = 8
    for g in range(n_pages // group):
        kx = jnp.concatenate([buf[slot, g * group + j] for j in range(group)], axis=1).astype(BF16)
        s = jnp.dot(qi, kx, preferred_element_type=F32)
        sc = jnp.sum(jnp.maximum(s, 0.0) * wi, axis=0, keepdims=True) * INDEX_SCALE
        o_ref[0, :, g * group * page:(g + 1) * group * page] = sc
    kn = kn_ref[0].astype(BF16).astype(F32)
    sn = jnp.sum(qi.astype(F32) * kn, axis=1, keepdims=True)
    s_new = jnp.sum(jnp.maximum(sn, 0.0) * wi, axis=0, keepdims=True) * INDEX_SCALE
    lane = lax.broadcasted_iota(I32, (1, LANES), 1)
    o_ref[0, :, past:past + LANES] = jnp.where(lane == 0, s_new, -jnp.inf)


def _sample_scores(page_table, qi, wi, kidx_new, cache_kidx_t, layer):
    nb, n_pages = page_table.shape
    page = cache_kidx_t.shape[3]
    past = n_pages * page
    assert n_pages % 8 == 0 and page == LANES
    grid_spec = pltpu.PrefetchScalarGridSpec(
        num_scalar_prefetch=1,
        grid=(nb,),
        in_specs=[pl.BlockSpec((1, N_IDX_HEADS, IDX_DIM), lambda i, pt: (i, 0, 0)),
                  pl.BlockSpec((1, N_IDX_HEADS, 1), lambda i, pt: (i, 0, 0)),
                  pl.BlockSpec((1, 1, IDX_DIM), lambda i, pt: (i, 0, 0)),
                  pl.BlockSpec(memory_space=pl.ANY)],
        out_specs=pl.BlockSpec((1, 1, past + LANES), lambda i, pt: (i, 0, 0)),
        scratch_shapes=[pltpu.VMEM((2, n_pages, IDX_DIM, page), F32), pltpu.SemaphoreType.DMA((2,))],
    )
    return pl.pallas_call(
        functools.partial(_sscore_kernel, layer=layer, n_pages=n_pages, page=page),
        grid_spec=grid_spec,
        out_shape=jax.ShapeDtypeStruct((nb, 1, past + LANES), F32),
        compiler_params=_params("arbitrary"),
        name="sample_scores",
    )(page_table, qi, wi, kidx_new, cache_kidx_t)


def _ssel_kernel(sc_ref, madd_ref, keys_ref, *, topk):
    nb, width = sc_ref.shape
    nblk = width // LANES
    keys_ref[...] = _sort_key(sc_ref[...])

    def count(pred):
        def body(c, acc):
            off = pl.multiple_of(c * LANES, LANES)
            return acc + jnp.where(pred(keys_ref[:, pl.ds(off, LANES)]), 1, 0)
        acc = lax.fori_loop(0, nblk, body, jnp.zeros((nb, LANES), I32))
        return jnp.sum(acc, axis=1, keepdims=True)

    def bit_body(i, kk):
        cand = kk + (jnp.int32(1) << (31 - i))
        return jnp.where(count(lambda blk: blk >= cand) >= topk, cand, kk)

    kth = lax.fori_loop(0, 32, bit_body, jnp.full((nb, 1), INT_MIN, I32))
    need = (topk - count(lambda blk: blk > kth)).astype(F32)
    r_i = lax.broadcasted_iota(I32, (LANES, LANES), 0)
    c_i = lax.broadcasted_iota(I32, (LANES, LANES), 1)
    upper = jnp.where(r_i < c_i, 1.0, 0.0).astype(BF16)

    def mask_block(c, eqc):
        off = pl.multiple_of(c * LANES, LANES)
        sk = keys_ref[:, pl.ds(off, LANES)]
        eq = sk == kth
        eqf = jnp.where(eq, 1.0, 0.0)
        rank = jnp.dot(eqf.astype(BF16), upper, preferred_element_type=F32) + eqc
        madd_ref[:, pl.ds(off, LANES)] = jnp.where(
            sk > kth, 0.0, jnp.where(eq, jnp.where(rank < need, 0.0, NEG_BIG), NEG_BIG))
        return eqc + jnp.sum(eqf, axis=1, keepdims=True)

    lax.fori_loop(0, nblk, mask_block, jnp.zeros((nb, 1), F32))


def _sample_select(scores, topk):
    nb, width = scores.shape
    return pl.pallas_call(
        functools.partial(_ssel_kernel, topk=topk),
        out_shape=jax.ShapeDtypeStruct((nb, width), F32),
        scratch_shapes=[pltpu.VMEM((nb, width), I32)],
        compiler_params=pltpu.CompilerParams(vmem_limit_bytes=VMEM_LIMIT_BYTES),
        name="sample_select",
    )(scores)


def _sattn_kernel(pt_ref, q_ref, qt_ref, kn_ref, vnt_ref, madd_ref, bias_ref, ck_ref, cv_ref, o_ref,
                  kbuf, vbuf, qb_ref, acc_ref, m_ref, l_ref, sem, *, layer, cp, page, past):
    b = pl.program_id(0)
    c = pl.program_id(1)
    nc = pl.num_programs(1)
    step = b * nc + c
    n_steps = pl.num_programs(0) * nc

    def page_copies(bb, cc, slot, j):
        phys = pt_ref[bb, cc * cp + j]
        return (pltpu.make_async_copy(ck_ref.at[layer, phys], kbuf.at[slot, j], sem.at[0, slot]),
                pltpu.make_async_copy(cv_ref.at[layer, phys], vbuf.at[slot, j], sem.at[1, slot]))

    def fetch(bb, cc, slot):
        for j in range(cp):
            ck, cv = page_copies(bb, cc, slot, j)
            ck.start()
            cv.start()

    @pl.when(step == 0)
    def _():
        fetch(0, 0, 0)

    nxt = step + 1

    @pl.when(nxt < n_steps)
    def _():
        fetch(nxt // nc, nxt % nc, nxt % 2)

    slot = step % 2
    for j in range(cp):
        ck, cv = page_copies(b, c, slot, j)
        ck.wait()
        cv.wait()

    @pl.when(c == 0)
    def _():
        m_ref[...] = jnp.full(m_ref.shape, NEG_BIG, F32)
        l_ref[...] = jnp.zeros(l_ref.shape, F32)
        acc_ref[...] = jnp.zeros(acc_ref.shape, F32)
        qt = qt_ref[0]
        for h in range(N_HEADS):
            qb_ref[h] = _lane_rep(qt[:, h:h + 1], page)

    pages = []
    for j in range(cp):
        rows = [jnp.sum(kbuf[slot, j, h] * qb_ref[h], axis=0, keepdims=True) for h in range(N_HEADS)]
        pages.append(jnp.concatenate(rows, axis=0))
    width = cp * page
    col0 = pl.multiple_of(c * width, width)
    s = jnp.concatenate(pages, axis=1) + bias_ref[:, pl.ds(col0, width)] + madd_ref[0, :, pl.ds(col0, width)]
    m_old = m_ref[...]
    m_new = jnp.maximum(m_old, _lane_rep(jnp.max(s, axis=1, keepdims=True)))
    alpha = jnp.exp(m_old - m_new)
    p = jnp.exp(s - m_new[:, 0:1])
    l_ref[...] = alpha * l_ref[...] + _lane_rep(jnp.sum(p, axis=1, keepdims=True))
    m_ref[...] = m_new
    for h in range(N_HEADS):
        acc = acc_ref[h] * jnp.broadcast_to(alpha[h:h + 1, :], (HEAD_DIM, page))
        for j in range(cp):
            acc = acc + vbuf[slot, j, h] * jnp.broadcast_to(p[h:h + 1, j * page:(j + 1) * page], (HEAD_DIM, page))
        acc_ref[h] = acc

    @pl.when(c == nc - 1)
    def _():
        s_new = (jnp.sum(q_ref[0] * kn_ref[0], axis=1, keepdims=True)
                 + bias_ref[:, past:past + 1] + madd_ref[0, :, past:past + 1])
        m_prev = m_ref[...]
        m_fin = jnp.maximum(m_prev, s_new)
        a_fin = jnp.exp(m_prev - m_fin)
        p_new = jnp.exp(s_new - m_fin)
        l_fin = a_fin * l_ref[...] + p_new
        lane = lax.broadcasted_iota(I32, (HEAD_DIM, LANES), 1)
        out = jnp.zeros((HEAD_DIM, LANES), F32)
        vnt = vnt_ref[0]
        for h in range(N_HEADS):
            tot = jnp.sum(acc_ref[h], axis=1, keepdims=True)
            col = (tot * a_fin[h:h + 1, 0:1] + p_new[h:h + 1, 0:1] * vnt[:, h:h + 1]) / l_fin[h:h + 1, 0:1]
            out = jnp.where(lane == h, col, out)
        o_ref[0] = out


def _sample_attention(page_table, q, k_new, v_new, madd, bias_s, cache_k_t, cache_v_t, layer, *, pages_per_step):
    nb, n_pages = page_table.shape
    page = cache_k_t.shape[4]
    past = n_pages * page
    cp = pages_per_step
    assert n_pages % cp == 0 and page == LANES
    head_blk = pl.BlockSpec((1, N_HEADS, HEAD_DIM), lambda i, j, pt: (i, 0, 0))
    tr_blk = pl.BlockSpec((1, HEAD_DIM, N_HEADS), lambda i, j, pt: (i, 0, 0))
    grid_spec = pltpu.PrefetchScalarGridSpec(
        num_scalar_prefetch=1,
        grid=(nb, n_pages // cp),
        in_specs=[head_blk, tr_blk, head_blk, tr_blk,
                  pl.BlockSpec((1, 1, past + LANES), lambda i, j, pt: (i, 0, 0)),
                  pl.BlockSpec(bias_s.shape, lambda i, j, pt: (0, 0)),
                  pl.BlockSpec(memory_space=pl.ANY), pl.BlockSpec(memory_space=pl.ANY)],
        out_specs=pl.BlockSpec((1, HEAD_DIM, LANES), lambda i, j, pt: (i, 0, 0)),
        scratch_shapes=[pltpu.VMEM((2, cp, N_HEADS, HEAD_DIM, page), F32),
                        pltpu.VMEM((2, cp, N_HEADS, HEAD_DIM, page), F32),
                        pltpu.VMEM((N_HEADS, HEAD_DIM, page), F32), pltpu.VMEM((N_HEADS, HEAD_DIM, page), F32),
                        pltpu.VMEM((N_HEADS, LANES), F32), pltpu.VMEM((N_HEADS, LANES), F32),
                        pltpu.SemaphoreType.DMA((2, 2))],
    )
    return pl.pallas_call(
        functools.partial(_sattn_kernel, layer=layer, cp=cp, page=page, past=past),
        grid_spec=grid_spec,
        out_shape=jax.ShapeDtypeStruct((nb, HEAD_DIM, LANES), F32),
        compiler_params=_params("arbitrary", "arbitrary"),
        name="sample_attention",
    )(page_table, q, jnp.swapaxes(q, 1, 2), k_new, jnp.swapaxes(v_new, 1, 2), madd, bias_s, cache_k_t, cache_v_t)


def _mix_kernel(x_ref, ya_ref, mc_ref, gb_ref, ga1_ref, sc2_ref, sh2_ref, gpost_ref, gpre_ref,
                wao_ref, wmo_ref, wr_ref, br_ref, x1_ref, h2_ref, te_ref, gates_ref):
    x = x_ref[0]
    t, d = x.shape
    attn = jnp.dot(ya_ref[0].astype(BF16), wao_ref[...], preferred_element_type=F32)
    merged = mc_ref[0] + gb_ref[0] * attn
    z = jnp.dot(merged.astype(BF16), wmo_ref[...], preferred_element_type=F32)
    x1 = x + ga1_ref[0] * _rms(z, gpost_ref[...])
    x1_ref[0] = x1
    h2 = _rms(x1, gpre_ref[...]) * (1.0 + sc2_ref[0]) + sh2_ref[0]
    for j in range(d // LANES):
        h2_ref[:, j, :] = h2[:, j * LANES:(j + 1) * LANES]
    lane = lax.broadcasted_iota(I32, (t, LANES), 1)
    logits = jnp.dot(h2.astype(BF16), wr_ref[...], preferred_element_type=F32) + br_ref[...]
    work = jnp.where(lane < N_EXPERTS, logits, -jnp.inf)
    top_l, top_e = [], []
    for _ in range(TOP_K_EXPERTS):
        mk = jnp.max(work, axis=1, keepdims=True)
        ek = jnp.min(jnp.where(work == mk, lane, LANES), axis=1, keepdims=True)
        top_l.append(mk)
        top_e.append(ek)
        work = jnp.where(lane == ek, -jnp.inf, work)
    ex = [jnp.exp(tl - top_l[0]) for tl in top_l]
    denom = ex[0] + ex[1] + ex[2] + ex[3]
    te = jnp.zeros((t, LANES), I32)
    gates = jnp.zeros((t, LANES), F32)
    for k in range(TOP_K_EXPERTS):
        te = jnp.where(lane == k, top_e[k], te)
        gates = jnp.where(lane == k, ex[k] / denom, gates)
    te_ref[...] = te
    gates_ref[...] = gates


def _mix_and_route(x, ya, mc, gb, ga1, sc2, sh2, gpost, gpre, wao, wmo, wr, br, *, rows):
    b, t, d = x.shape
    nblk = t // rows
    per_row = ga1.shape[1] != 1

    def row_spec(width, rowwise=True):
        if rowwise:
            return pl.BlockSpec((1, rows, width), lambda i, j: (i, j, 0))
        return pl.BlockSpec((1, 1, width), lambda i, j: (i, 0, 0))

    flat = lambda w: pl.BlockSpec((rows, w), lambda i, j: (i * nblk + j, 0))
    n = b * t
    return pl.pallas_call(
        _mix_kernel,
        grid=(b, nblk),
        in_specs=[row_spec(d), row_spec(D_ATTN), row_spec(d), row_spec(d),
                  row_spec(d, per_row), row_spec(d, per_row), row_spec(d, per_row),
                  _const_spec(gpost.shape), _const_spec(gpre.shape), _const_spec(wao.shape),
                  _const_spec(wmo.shape), _const_spec(wr.shape), _const_spec(br.shape)],
        out_specs=[row_spec(d),
                   pl.BlockSpec((rows, d // LANES, LANES), lambda i, j: (i * nblk + j, 0, 0)),
                   flat(LANES), flat(LANES)],
        out_shape=[jax.ShapeDtypeStruct((b, t, d), F32),
                   jax.ShapeDtypeStruct((n, d // LANES, LANES), F32),
                   jax.ShapeDtypeStruct((n, LANES), I32),
                   jax.ShapeDtypeStruct((n, LANES), F32)],
        compiler_params=_params("arbitrary", "arbitrary"),
        name="mix_and_route",
    )(x, ya, mc, gb, ga1, sc2, sh2, gpost, gpre, wao, wmo, wr, br)


def _rank_kernel(te_ref, rank_ref, cnt_ref, carry_ref):
    @pl.when(pl.program_id(0) == 0)
    def _():
        carry_ref[...] = jnp.zeros(carry_ref.shape, F32)

    te = te_ref[...]
    t = te.shape[0]
    lane = lax.broadcasted_iota(I32, (t, LANES), 1)
    hits = [lane == te[:, k:k + 1] for k in range(TOP_K_EXPERTS)]
    onehot = jnp.zeros((t, LANES), F32)
    for hit in hits:
        onehot = onehot + jnp.where(hit, 1.0, 0.0)
    lower = jnp.where(lax.broadcasted_iota(I32, (t, t), 0) > lax.broadcasted_iota(I32, (t, t), 1), 1.0, 0.0)
    before = jnp.dot(lower.astype(BF16), onehot.astype(BF16), preferred_element_type=F32) + carry_ref[...]
    rank = jnp.zeros((t, LANES), F32)
    for k, hit in enumerate(hits):
        rk = jnp.sum(jnp.where(hit, before, 0.0), axis=1, keepdims=True)
        rank = jnp.where(lane == k, rk, rank)
    rank_ref[...] = rank.astype(I32)
    carry_ref[...] = carry_ref[...] + jnp.sum(onehot, axis=0, keepdims=True)
    cnt_ref[...] = jnp.broadcast_to(carry_ref[...], cnt_ref.shape)


def _expert_ranks(te, *, rows):
    n = te.shape[0]
    return pl.pallas_call(
        _rank_kernel,
        grid=(n // rows,),
        in_specs=[pl.BlockSpec((rows, LANES), lambda i: (i, 0))],
        out_specs=[pl.BlockSpec((rows, LANES), lambda i: (i, 0)),
                   pl.BlockSpec((SUBLANES, LANES), lambda i: (0, 0))],
        out_shape=[jax.ShapeDtypeStruct((n, LANES), I32), jax.ShapeDtypeStruct((SUBLANES, LANES), F32)],
        scratch_shapes=[pltpu.VMEM((1, LANES), F32)],
        compiler_params=_params("arbitrary"),
        name="expert_ranks",
    )(te)


def _scatter_kernel(dest_ref, h2_ref, init_ref, xs_ref, sem, *, rows):
    del init_ref
    base = pl.program_id(0) * rows * TOP_K_EXPERTS

    def row_copy(n, k):
        return pltpu.make_async_copy(h2_ref.at[n], xs_ref.at[dest_ref[base + n * TOP_K_EXPERTS + k]], sem)

    def start(n, carry):
        for k in range(TOP_K_EXPERTS):
            row_copy(n, k).start(priority=k % 2)
        return carry

    def wait(n, carry):
        for k in range(TOP_K_EXPERTS):
            row_copy(n, k).wait()
        return carry

    lax.fori_loop(0, rows, start, 0)
    lax.fori_loop(0, rows, wait, 0)


def _scatter_rows(dest_flat, h2, cap, *, rows):
    n, s, _ = h2.shape
    init = jnp.zeros((cap, s, LANES), F32)
    grid_spec = pltpu.PrefetchScalarGridSpec(
        num_scalar_prefetch=1,
        grid=(n // rows,),
        in_specs=[pl.BlockSpec((rows, s, LANES), lambda i, d: (i, 0, 0)),
                  pl.BlockSpec(memory_space=pl.ANY)],
        out_specs=pl.BlockSpec(memory_space=pl.ANY),
        scratch_shapes=[pltpu.SemaphoreType.DMA(())],
    )
    return pl.pallas_call(
        functools.partial(_scatter_kernel, rows=rows),
        grid_spec=grid_spec,
        out_shape=jax.ShapeDtypeStruct((cap, s, LANES), F32),
        input_output_aliases={2: 0},
        compiler_params=_params("arbitrary"),
        name="scatter_rows",
    )(dest_flat, h2, init)


def _expert_kernel(be_ref, nu_ref, xs_ref, wgu_ref, bgu_ref, wdn_ref, bdn_ref, yb_ref):
    del be_ref

    @pl.when(pl.program_id(0) < nu_ref[0])
    def _():
        nseg = xs_ref.shape[1]
        x = jnp.concatenate([xs_ref[:, j, :] for j in range(nseg)], axis=1).astype(BF16)
        h = jnp.dot(x, wgu_ref[0], preferred_element_type=F32) + bgu_ref[0]
        gate = jnp.minimum(h[:, :D_FF], SWIGLU_LIMIT)
        up = jnp.clip(h[:, D_FF:], -SWIGLU_LIMIT, SWIGLU_LIMIT)
        act = (up + 1.0) * gate * jax.nn.sigmoid(SWIGLU_ALPHA * gate)
        y = jnp.dot(act.astype(BF16), wdn_ref[0], preferred_element_type=F32) + bdn_ref[0]
        for j in range(nseg):
            yb_ref[:, j, :] = y[:, j * LANES:(j + 1) * LANES]

    @pl.when(pl.program_id(0) >= nu_ref[0])
    def _():
        yb_ref[...] = jnp.zeros(yb_ref.shape, yb_ref.dtype)


def _expert_ffn(block_e, n_used, xs, w_gu, b_gu, w_dn, b_dn):
    cap, s, _ = xs.shape
    d = s * LANES
    nblk = cap // MOE_ROWS
    rows_map = lambda i, be, nu: (jnp.minimum(i, nu[0] - 1), 0, 0)
    by_expert = lambda i, be, nu: (be[i], 0, 0)
    grid_spec = pltpu.PrefetchScalarGridSpec(
        num_scalar_prefetch=2,
        grid=(nblk,),
        in_specs=[pl.BlockSpec((MOE_ROWS, s, LANES), rows_map),
                  pl.BlockSpec((1, d, 2 * D_FF), by_expert),
                  pl.BlockSpec((1, 1, 2 * D_FF), by_expert),
                  pl.BlockSpec((1, D_FF, d), by_expert),
                  pl.BlockSpec((1, 1, d), by_expert)],
        out_specs=pl.BlockSpec((MOE_ROWS, s, LANES), lambda i, be, nu: (i, 0, 0)),
    )
    return pl.pallas_call(
        _expert_kernel,
        grid_spec=grid_spec,
        out_shape=jax.ShapeDtypeStruct((cap, s, LANES), F32),
        compiler_params=_params("arbitrary"),
        name="expert_ffn",
    )(block_e, n_used, xs, w_gu, b_gu, w_dn, b_dn)


def _combine_kernel(dest_ref, gates_ref, x1_ref, ga2_ref, g_ref, yb_ref, o_ref, buf, sem, *, rows):
    nblk = pl.num_programs(1)
    step = pl.program_id(0) * nblk + pl.program_id(1)
    n_steps = pl.num_programs(0) * nblk

    def row_copy(st, slot, n, k):
        src = dest_ref[(st * rows + n) * TOP_K_EXPERTS + k]
        return pltpu.make_async_copy(yb_ref.at[src], buf.at[slot, k, n], sem.at[slot])

    def fetch(st, slot):
        def start(n, carry):
            for k in range(TOP_K_EXPERTS):
                row_copy(st, slot, n, k).start(priority=k % 2)
            return carry

        lax.fori_loop(0, rows, start, 0)

    @pl.when(step == 0)
    def _():
        fetch(0, 0)

    @pl.when(step + 1 < n_steps)
    def _():
        fetch(step + 1, (step + 1) % 2)

    slot = step % 2

    def wait(n, carry):
        for k in range(TOP_K_EXPERTS):
            row_copy(step, slot, n, k).wait()
        return carry

    lax.fori_loop(0, rows, wait, 0)
    gates = gates_ref[...]
    nseg = buf.shape[3]
    f = None
    for k in range(TOP_K_EXPERTS):
        yk = jnp.concatenate([buf[slot, k, :, j, :] for j in range(nseg)], axis=1)
        term = yk * gates[:, k:k + 1]
        f = term if f is None else f + term
    o_ref[0] = x1_ref[0] + ga2_ref[0] * _rms(f, g_ref[...])


def _combine(dest_flat, gates, x1, ga2, g, yb, *, rows):
    b, t, d = x1.shape
    nblk = t // rows
    per_row = ga2.shape[1] != 1
    row_blk = pl.BlockSpec((1, rows, d), lambda i, j, ds: (i, j, 0))
    mod_blk = row_blk if per_row else pl.BlockSpec((1, 1, d), lambda i, j, ds: (i, 0, 0))
    grid_spec = pltpu.PrefetchScalarGridSpec(
        num_scalar_prefetch=1,
        grid=(b, nblk),
        in_specs=[pl.BlockSpec((rows, LANES), lambda i, j, ds: (i * nblk + j, 0)),
                  row_blk, mod_blk,
                  pl.BlockSpec(g.shape, lambda i, j, ds: (0, 0)),
                  pl.BlockSpec(memory_space=pl.ANY)],
        out_specs=row_blk,
        scratch_shapes=[pltpu.VMEM((2, TOP_K_EXPERTS, rows, d // LANES, LANES), F32),
                        pltpu.SemaphoreType.DMA((2,))],
    )
    return pl.pallas_call(
        functools.partial(_combine_kernel, rows=rows),
        grid_spec=grid_spec,
        out_shape=jax.ShapeDtypeStruct((b, t, d), F32),
        compiler_params=_params("arbitrary", "arbitrary"),
        name="combine",
    )(dest_flat, gates, x1, ga2, g, yb)


def _moe(x1, h2, te, gates, ga2, g_post, w_gu, b_gu, w_dn, b_dn, *, rank_rows, io_rows):
    n = h2.shape[0]
    n_slots = n * TOP_K_EXPERTS
    rank, counts = _expert_ranks(te, rows=rank_rows)
    counts = counts[0, :N_EXPERTS].astype(I32)
    padded = (counts + MOE_ROWS - 1) // MOE_ROWS * MOE_ROWS
    pend = jnp.cumsum(padded)
    pstart = pend - padded
    nblk = (n_slots + N_EXPERTS * (MOE_ROWS - 1) + MOE_ROWS - 1) // MOE_ROWS
    n_used = (pend[-1] // MOE_ROWS).astype(I32)
    blk_start = jnp.arange(nblk, dtype=I32) * MOE_ROWS
    block_e = jnp.minimum(jnp.sum(pend[None, :] <= blk_start[:, None], axis=1), N_EXPERTS - 1).astype(I32)
    block_e = jnp.where(jnp.arange(nblk) < n_used, block_e, block_e[jnp.maximum(n_used - 1, 0)])
    top_e = te[:, :TOP_K_EXPERTS]
    dest = (pstart[top_e] + rank[:, :TOP_K_EXPERTS]).reshape(-1).astype(I32)
    xs = _scatter_rows(dest, h2, nblk * MOE_ROWS, rows=io_rows)
    yb = _expert_ffn(block_e, n_used.reshape(1), xs, w_gu, b_gu, w_dn, b_dn)
    return _combine(dest, gates, x1, ga2, g_post, yb, rows=io_rows)


def _t5_bucket(dist):
    n = jnp.maximum(dist, 0)
    max_exact = N_BUCKETS // 2
    nf = jnp.maximum(n, 1).astype(F32)
    large = max_exact + (jnp.log(nf / max_exact) / math.log(MAX_DISTANCE / max_exact)
                         * (N_BUCKETS - max_exact)).astype(I32)
    large = jnp.minimum(large, N_BUCKETS - 1)
    return jnp.where(n < max_exact, n, large)


def _split_in_proj(w_in):
    d = w_in.shape[0]
    wa = w_in[:, :OFF_KI].astype(BF16)
    wk = jnp.pad(w_in[:, OFF_KI:OFF_GA], ((0, 0), (0, LANES - (OFF_GA - OFF_KI)))).astype(BF16)
    wga = w_in[:, OFF_GA:OFF_GA + d].astype(BF16)
    wgb = w_in[:, OFF_GA + d:OFF_GA + 2 * d].astype(BF16)
    return wa, wk, wga, wgb


def kernel(x_prompt, x_sample, cache_k, cache_v, cache_kidx, state_conv, page_table, c_prompt, c_sample, rel_bias, w_mod, b_mod, g_pre_mix, g_post_mix, w_in, conv_w, w_conv_out, w_attn_out, w_mix_out, g_pre_ffn, g_post_ffn, w_router, b_router, w_gu, b_gu, w_dn, b_dn):
    depth = w_mod.shape[0]
    bp, seq, d = x_prompt.shape
    nb, dec_seq, _ = x_sample.shape
    assert dec_seq == 1
    page = cache_k.shape[2]
    past = page_table.shape[1] * page
    tq, tk = 512, 512
    rows_p = 256
    s_topk = min(TOPK_MAX, (past + dec_seq) // 4)

    tab = rel_bias.astype(F32)[_t5_bucket(jnp.arange(2 * LANES + 1, dtype=I32))].T
    rel = tab - tab[:, MAX_DISTANCE:MAX_DISTANCE + 1]
    ii = jnp.arange(LANES, dtype=I32)[:, None]
    jj = jnp.arange(LANES, dtype=I32)[None, :]
    bias_tiles = jnp.stack([rel[:, jnp.clip(ii - jj, 0, 2 * LANES)],
                            rel[:, jnp.clip(LANES + ii - jj, 0, 2 * LANES)]], axis=1)
    bias_s = tab[:, jnp.clip(past - jnp.arange(past + LANES, dtype=I32), 0, MAX_DISTANCE)]
    cache_k_t = jnp.transpose(cache_k, (0, 1, 3, 4, 2))
    cache_v_t = jnp.transpose(cache_v, (0, 1, 3, 4, 2))
    cache_kidx_t = jnp.transpose(cache_kidx, (0, 1, 3, 2))

    xp = x_prompt
    xs_rows = x_sample.reshape(1, nb, d)
    c_all = jnp.concatenate([c_prompt, c_sample], axis=0)
    n_c = c_all.shape[0]
    c_all = jnp.pad(c_all, ((0, -n_c % SUBLANES), (0, 0)))
    outs = [[] for _ in range(8)]
    for l in range(depth):
        mod = _modulation(c_all, w_mod[l].astype(BF16), b_mod[l])
        mod_p = [m[:, None, :] for m in jnp.split(mod[:bp], 6, axis=-1)]
        mod_s = [m[None] for m in jnp.split(mod[bp:bp + nb], 6, axis=-1)]
        wa, wk, wga, wgb = _split_in_proj(w_in[l])
        wts = (wa, wk, wga, wgb, conv_w[l], w_conv_out[l].astype(BF16))
        g1 = g_pre_mix[l].reshape(1, d)
        wao = w_attn_out[l].astype(BF16)
        wmo = w_mix_out[l].astype(BF16)
        wr = jnp.pad(w_router[l], ((0, 0), (0, LANES - N_EXPERTS))).astype(BF16)
        br = jnp.pad(b_router[l], (0, LANES - N_EXPERTS)).reshape(1, LANES)
        gpost = g_post_mix[l].reshape(1, d)
        gpre2 = g_pre_ffn[l].reshape(1, d)
        gpost2 = g_post_ffn[l].reshape(1, d)
        wgu = w_gu[l].astype(BF16)
        bgu = b_gu[l].reshape(N_EXPERTS, 1, 2 * D_FF)
        wdn = w_dn[l].astype(BF16)
        bdn = b_dn[l].reshape(N_EXPERTS, 1, d)

        st = state_conv[l].reshape(1, nb, 2 * D_CONV)
        q, k, v, qi, kiwi, mc, gb, u = _in_projection(
            xs_rows, mod_s[1], mod_s[0], st, g1, wts, seq_conv=False, rows=nb)
        kidx = kiwi[0, :, :IDX_DIM]
        wi = kiwi[0, :, IDX_DIM:IDX_DIM + N_IDX_HEADS]
        scores = _sample_scores(page_table, qi.reshape(nb, N_IDX_HEADS, IDX_DIM), wi[:, :, None],
                                kidx[:, None, :], cache_kidx_t, l)
        madd = _sample_select(scores.reshape(nb, past + LANES), s_topk)
        k3 = k.reshape(nb, N_HEADS, HEAD_DIM)
        v3 = v.reshape(nb, N_HEADS, HEAD_DIM)
        ya = _sample_attention(page_table, q.astype(F32).reshape(nb, N_HEADS, HEAD_DIM), k3, v3,
                               madd.reshape(nb, 1, past + LANES), bias_s, cache_k_t, cache_v_t, l,
                               pages_per_step=16)
        ya = jnp.swapaxes(ya[:, :, :N_HEADS], 1, 2)
        x1, h2, te, gates = _mix_and_route(xs_rows, ya.reshape(1, nb, D_ATTN), mc, gb, mod_s[2], mod_s[4],
                                           mod_s[3], gpost, gpre2, wao, wmo, wr, br, rows=nb)
        xs_rows = _moe(x1, h2, te, gates, mod_s[5], gpost2, wgu, bgu, wdn, bdn, rank_rows=nb, io_rows=nb)
        outs[4].append(k.reshape(nb, dec_seq, N_HEADS, HEAD_DIM))
        outs[5].append(v.reshape(nb, dec_seq, N_HEADS, HEAD_DIM))
        outs[6].append(kidx[:, None, :])
        outs[7].append(jnp.stack([state_conv[l][:, 1, :], u[0]], axis=1))

        prefix = jnp.zeros((bp, 1, 2 * D_CONV), F32)
        q, k, v, qi, kiwi, mc, gb, u_tail = _in_projection(
            xp, mod_p[1], mod_p[0], prefix, g1, wts, seq_conv=True, rows=rows_p)
        kidx = kiwi[..., :IDX_DIM]
        by_head = lambda a: jnp.swapaxes(a.reshape(bp, seq, N_HEADS, HEAD_DIM), 1, 2)
        ya = _prompt_attention(by_head(q), by_head(qi), kiwi, jnp.swapaxes(kidx, 1, 2).astype(BF16),
                               (jnp.swapaxes(k, 1, 2) * LOG2E).astype(BF16), v.astype(BF16),
                               bias_tiles * LOG2E, tq=tq, tk=tk)
        x1, h2, te, gates = _mix_and_route(xp, ya, mc, gb, mod_p[2], mod_p[4], mod_p[3], gpost, gpre2,
                                           wao, wmo, wr, br, rows=rows_p)
        xp = _moe(x1, h2, te, gates, mod_p[5], gpost2, wgu, bgu, wdn, bdn, rank_rows=512, io_rows=rows_p)
        outs[0].append(k.reshape(bp, seq, N_HEADS, HEAD_DIM))
        outs[1].append(v.reshape(bp, seq, N_HEADS, HEAD_DIM))
        outs[2].append(kidx)
        outs[3].append(u_tail[:, SUBLANES - (CONV_W - 1):, :])
    return (xp, xs_rows.reshape(nb, dec_seq, d)) + tuple(jnp.stack(o) for o in outs)
```
